```python
import jax, jax.numpy as jnp
from jax import lax
import numpy as np

D_MODEL = 2048
BATCH = 2
SEQ = 4096
DEPTH = 4

N_A_LAYERS = DEPTH // 2
CHUNK = 128
A_WIDTH = D_MODEL
A_GROUPS = 8
A_GROUP_CH = A_WIDTH // A_GROUPS
HEAD_DIM = 128
B_HEADS = D_MODEL // HEAD_DIM
B_WINDOWS = (128, 512, 2048)
B_DILATIONS = (1, 4, 16)
B_GROUPS = len(B_WINDOWS)
QBLK = 128
N_EXPERTS = 32
TOP_K = 4
EXPERT_FF = D_MODEL // 2
SWIGLU_LIMIT = 7.0
SWIGLU_ALPHA = 1.702
MOE_BLOCK = 128
PLE_DIM = 256
LN_EPS = 1e-5
DEEPNORM_ALPHA = (2.0 * DEPTH) ** 0.25
DEEPNORM_BETA = (8.0 * DEPTH) ** -0.25
NEG_INF = -1e30

kernel_name = "yoco_gmlp_dilated_attn_moe_deepnorm"


def layer_norm(x, g, b):
    xf = x.astype(jnp.float32)
    mu = jnp.mean(xf, axis=-1, keepdims=True)
    var = jnp.mean(jnp.square(xf - mu), axis=-1, keepdims=True)
    y = (xf - mu) * lax.rsqrt(var + LN_EPS)
    return (y * g.astype(jnp.float32) + b.astype(jnp.float32)).astype(x.dtype)


def alibi_slopes(n_heads):
    return 2.0 ** (-8.0 * jnp.arange(1, n_heads + 1, dtype=jnp.float32) / n_heads)


def chunked_spatial_gating(x, w_in, b_in, ln_g, ln_b, w_s, b_s, w_out, b_out):
    bsz, seq, _ = x.shape
    z = jax.nn.gelu(x @ w_in + b_in, approximate=False)
    u, v = jnp.split(z, 2, axis=-1)
    v = layer_norm(v, ln_g, ln_b)
    v = v.reshape(bsz, seq // CHUNK, CHUNK, A_GROUPS, A_GROUP_CH)
    causal = jnp.tril(jnp.ones((CHUNK, CHUNK), dtype=bool))
    w = jnp.where(causal, w_s, 0.0).astype(v.dtype)
    vs = jnp.einsum('gts,bnsgc->bntgc', w, v) + b_s.T[:, :, None].astype(v.dtype)
    gated = u * vs.reshape(bsz, seq, A_WIDTH)
    return gated @ w_out + b_out


def dilated_branch(q, k, v, dilation, n_back, slopes):
    bsz, seq, nh, dh = q.shape
    sub_len = seq // dilation
    nb = -(-sub_len // QBLK)
    padded_len = nb * QBLK

    def to_blocks(t):
        t = t.reshape(bsz, sub_len, dilation, nh, dh).transpose(0, 2, 3, 1, 4)
        t = jnp.pad(t, ((0, 0), (0, 0), (0, 0), (0, padded_len - sub_len), (0, 0)))
        return t.reshape(bsz, dilation, nh, nb, QBLK, dh)

    def band(t):
        prev = jnp.pad(t[:, :, :, :-1], ((0, 0), (0, 0), (0, 0), (1, 0), (0, 0), (0, 0)))
        return jnp.concatenate([prev, t], axis=4)

    qb = to_blocks(q)
    kband = band(to_blocks(k))
    vband = band(to_blocks(v)).astype(jnp.float32)
    scores = jnp.einsum('brhnqd,brhnkd->brhnqk', qb, kband,
                        preferred_element_type=jnp.float32) * (dh ** -0.5)
    qi = jnp.arange(QBLK)[:, None]
    ki = jnp.arange(2 * QBLK)[None, :]
    dist = qi + QBLK - ki
    key_idx = (jnp.arange(nb)[:, None, None] - 1) * QBLK + ki[None]
    mask = (dist >= 0) & (dist <= n_back) & (key_idx >= 0)
    bias = -slopes[:, None, None, None] * (dist * dilation).astype(jnp.float32)[None, None]
    scores = jnp.where(mask, scores + bias, NEG_INF)
    m = jnp.max(scores, axis=-1, keepdims=True)
    pexp = jnp.exp(scores - m)
    den = jnp.sum(pexp, axis=-1, keepdims=True)
    o = jnp.einsum('brhnqk,brhnkd->brhnqd', pexp, vband) / den
    lse = (m + jnp.log(den))[..., 0]
    o = o.reshape(bsz, dilation, nh, padded_len, dh)[:, :, :, :sub_len]
    o = o.transpose(0, 3, 1, 2, 4).reshape(bsz, seq, nh, dh)
    lse = lse.reshape(bsz, dilation, nh, padded_len)[:, :, :, :sub_len]
    lse = lse.transpose(0, 3, 1, 2).reshape(bsz, seq, nh)
    return o, lse


def dilated_attention(x, w_q, w_o, k, v):
    bsz, seq, _ = x.shape
    q = (x @ w_q).reshape(bsz, seq, B_GROUPS, B_HEADS, HEAD_DIM)
    slopes = alibi_slopes(B_HEADS)
    outs, lses = [], []
    for g in range(B_GROUPS):
        o, lse = dilated_branch(q[:, :, g], k, v, B_DILATIONS[g],
                                B_WINDOWS[g] // B_DILATIONS[g], slopes)
        outs.append(o)
        lses.append(lse)
    wts = jax.nn.softmax(jnp.stack(lses), axis=0)
    o = jnp.einsum('gbsh,gbshd->bshd', wts, jnp.stack(outs))
    return o.reshape(bsz, seq, B_HEADS * HEAD_DIM).astype(x.dtype) @ w_o


def moe_ffn(h, w_r, b_r, w_gu, b_gu, w_down, b_down):
    bsz, seq, d = h.shape
    n_tok = bsz * seq
    xt = h.reshape(n_tok, d)
    logits = (xt @ w_r + b_r).astype(jnp.float32)
    top_val, top_idx = lax.top_k(logits, TOP_K)
    gates = jax.nn.softmax(top_val, axis=-1)
    n_assign = n_tok * TOP_K
    expert_flat = top_idx.reshape(n_assign).astype(jnp.int32)
    token_flat = jnp.repeat(jnp.arange(n_tok, dtype=jnp.int32), TOP_K)
    gate_flat = gates.reshape(n_assign)
    e_sorted, tok_sorted, gate_sorted = lax.sort(
        (expert_flat, token_flat, gate_flat), num_keys=1, is_stable=True)
    counts = jnp.bincount(expert_flat, length=N_EXPERTS).astype(jnp.int32)
    starts = jnp.cumsum(counts) - counts
    padded = (counts + MOE_BLOCK - 1) // MOE_BLOCK * MOE_BLOCK
    padded_end = jnp.cumsum(padded)
    padded_start = padded_end - padded
    dest = padded_start[e_sorted] + jnp.arange(n_assign, dtype=jnp.int32) - starts[e_sorted]
    n_slots = n_assign + N_EXPERTS * MOE_BLOCK
    n_blocks = n_slots // MOE_BLOCK
    slot_tok = jnp.full((n_slots,), n_tok, jnp.int32).at[dest].set(tok_sorted)
    slot_gate = jnp.zeros((n_slots,), jnp.float32).at[dest].set(gate_sorted)
    block_expert = jnp.minimum(
        jnp.searchsorted(padded_end, jnp.arange(n_blocks, dtype=jnp.int32) * MOE_BLOCK, side='right'),
        N_EXPERTS - 1)
    x_pad = jnp.concatenate([xt, jnp.zeros((1, d), xt.dtype)], axis=0)

    def expert_block(args):
        tok, gate, e = args
        xb = x_pad[tok]
        gu = xb @ w_gu[e] + b_gu[e]
        g_, u_ = jnp.split(gu, 2, axis=-1)
        g_ = jnp.minimum(g_, SWIGLU_LIMIT)
        u_ = jnp.clip(u_, -SWIGLU_LIMIT, SWIGLU_LIMIT)
        y = ((u_ + 1.0) * (g_ * jax.nn.sigmoid(SWIGLU_ALPHA * g_))) @ w_down[e] + b_down[e]
        return y * gate[:, None].astype(y.dtype)

    ys = lax.map(expert_block, (slot_tok.reshape(n_blocks, MOE_BLOCK),
                                slot_gate.reshape(n_blocks, MOE_BLOCK), block_expert))
    out = jnp.zeros((n_tok + 1, d), h.dtype).at[slot_tok].add(ys.reshape(n_slots, d).astype(h.dtype))
    return out[:n_tok].reshape(bsz, seq, d)


def setup_inputs(seed: int = 0) -> dict:
    key = jax.random.key(seed)
    ks = iter(jax.random.split(key, 40))
    n_a = N_A_LAYERS
    n_b = DEPTH - N_A_LAYERS
    hd = B_HEADS * HEAD_DIM
    beta = DEEPNORM_BETA

    def nrm(shape, scale):
        return jax.random.normal(next(ks), shape, jnp.float32) * scale

    x = nrm((BATCH, SEQ, D_MODEL), 1.0)
    p = nrm((DEPTH, BATCH, SEQ, PLE_DIM), 1.0)
    a_w_in = nrm((n_a, D_MODEL, 2 * A_WIDTH), D_MODEL ** -0.5)
    a_b_in = nrm((n_a, 2 * A_WIDTH), 0.02)
    a_ln_g = 1.0 + nrm((n_a, A_WIDTH), 0.02)
    a_ln_b = nrm((n_a, A_WIDTH), 0.02)
    a_w_s = nrm((n_a, A_GROUPS, CHUNK, CHUNK), CHUNK ** -0.5)
    a_b_s = 1.0 + nrm((n_a, A_GROUPS, CHUNK), 0.02)
    a_w_out = nrm((n_a, A_WIDTH, D_MODEL), beta * A_WIDTH ** -0.5)
    a_b_out = nrm((n_a, D_MODEL), 0.02)
    kv_w = jnp.concatenate([nrm((D_MODEL, hd), D_MODEL ** -0.5),
                            nrm((D_MODEL, hd), beta * D_MODEL ** -0.5)], axis=1)
    b_w_q = nrm((n_b, D_MODEL, B_GROUPS * hd), D_MODEL ** -0.5)
    b_w_o = nrm((n_b, hd, D_MODEL), beta * hd ** -0.5)
    ln1_g = 1.0 + nrm((DEPTH, D_MODEL), 0.02)
    ln1_b = nrm((DEPTH, D_MODEL), 0.02)
    ln2_g = 1.0 + nrm((DEPTH, D_MODEL), 0.02)
    ln2_b = nrm((DEPTH, D_MODEL), 0.02)
    router_w = nrm((DEPTH, D_MODEL, N_EXPERTS), D_MODEL ** -0.5)
    router_b = nrm((DEPTH, N_EXPERTS), 0.01)
    moe_w_gu = nrm((DEPTH, N_EXPERTS, D_MODEL, 2 * EXPERT_FF), D_MODEL ** -0.5)
    moe_b_gu = nrm((DEPTH, N_EXPERTS, 2 * EXPERT_FF), 0.02)
    moe_w_down = nrm((DEPTH, N_EXPERTS, EXPERT_FF, D_MODEL), beta * EXPERT_FF ** -0.5)
    moe_b_down = nrm((DEPTH, N_EXPERTS, D_MODEL), 0.02)
    ple_w_p = nrm((DEPTH, PLE_DIM, D_MODEL), beta * PLE_DIM ** -0.5)
    ple_w_g = nrm((DEPTH, D_MODEL, D_MODEL), D_MODEL ** -0.5)
    ple_b_g = nrm((DEPTH, D_MODEL), 0.02)
    return {"x": x, "p": p, "a_w_in": a_w_in, "a_b_in": a_b_in, "a_ln_g": a_ln_g,
            "a_ln_b": a_ln_b, "a_w_s": a_w_s, "a_b_s": a_b_s, "a_w_out": a_w_out,
            "a_b_out": a_b_out, "kv_w": kv_w, "b_w_q": b_w_q, "b_w_o": b_w_o,
            "ln1_g": ln1_g, "ln1_b": ln1_b, "ln2_g": ln2_g, "ln2_b": ln2_b,
            "router_w": router_w, "router_b": router_b, "moe_w_gu": moe_w_gu,
            "moe_b_gu": moe_b_gu, "moe_w_down": moe_w_down, "moe_b_down": moe_b_down,
            "ple_w_p": ple_w_p, "ple_w_g": ple_w_g, "ple_b_g": ple_b_g}


def reference(x, p, a_w_in, a_b_in, a_ln_g, a_ln_b, a_w_s, a_b_s, a_w_out, a_b_out,
              kv_w, b_w_q, b_w_o, ln1_g, ln1_b, ln2_g, ln2_b, router_w, router_b,
              moe_w_gu, moe_b_gu, moe_w_down, moe_b_down, ple_w_p, ple_w_g, ple_b_g):
    bsz, seq, _ = x.shape
    k_sh = None
    v_sh = None
    for i in range(DEPTH):
        if i < N_A_LAYERS:
            mix = chunked_spatial_gating(x, a_w_in[i], a_b_in[i], a_ln_g[i], a_ln_b[i],
                                         a_w_s[i], a_b_s[i], a_w_out[i], a_b_out[i])
        else:
            j = i - N_A_LAYERS
            mix = dilated_attention(x, b_w_q[j], b_w_o[j], k_sh, v_sh)
        h = layer_norm(DEEPNORM_ALPHA * x + mix, ln1_g[i], ln1_b[i])
        ffn = moe_ffn(h, router_w[i], router_b[i], moe_w_gu[i], moe_b_gu[i],
                      moe_w_down[i], moe_b_down[i])
        ple = jax.nn.sigmoid(h @ ple_w_g[i] + ple_b_g[i]) * (p[i].astype(h.dtype) @ ple_w_p[i])
        x = layer_norm(DEEPNORM_ALPHA * h + ffn + ple, ln2_g[i], ln2_b[i])
        if i == N_A_LAYERS - 1:
            kv = (x @ kv_w).reshape(bsz, seq, 2, B_HEADS, HEAD_DIM)
            k_sh = kv[:, :, 0]
            v_sh = kv[:, :, 1]
    return x
```

```python
import functools

import jax
import jax.numpy as jnp
import numpy as np
from jax import lax
from jax.experimental import pallas as pl
from jax.experimental.pallas import tpu as pltpu

D_MODEL = 2048
DEPTH = 4
N_A_LAYERS = DEPTH // 2
CHUNK = 128
A_WIDTH = D_MODEL
A_GROUPS = 8
A_GROUP_CH = A_WIDTH // A_GROUPS
HEAD_DIM = 128
B_HEADS = D_MODEL // HEAD_DIM
B_WINDOWS = (128, 512, 2048)
B_DILATIONS = (1, 4, 16)
B_GROUPS = len(B_WINDOWS)
QBLK = 128
N_EXPERTS = 32
TOP_K = 4
EXPERT_FF = D_MODEL // 2
SWIGLU_LIMIT = 7.0
SWIGLU_ALPHA = 1.702
PLE_DIM = 256
LN_EPS = 1e-5
DEEPNORM_ALPHA = (2.0 * DEPTH) ** 0.25
NEG_INF = -1e30

V7X_VMEM_BYTES = 64 * 1024 * 1024
VMEM_LIMIT = 56 * 1024 * 1024
GF_TM, GF_TK = 512, 512
PL_TM = 256
MM_TM, MM_TN = 1024, 512
ATT_T = QBLK * max(B_DILATIONS)
MOE_F = 256
MOE_J = EXPERT_FF // MOE_F
MOE_SUB = 256
MOE_RMAX = 1280
C2_TM = 256

_BF = jnp.bfloat16
_F32 = jnp.float32


def _params(sem):
    return pltpu.CompilerParams(dimension_semantics=sem, vmem_limit_bytes=VMEM_LIMIT)


def _ln(y, g, b):
    mu = jnp.mean(y, axis=-1, keepdims=True)
    yc = y - mu
    var = jnp.mean(yc * yc, axis=-1, keepdims=True)
    return yc * lax.rsqrt(var + LN_EPS) * g + b


def _gmlp_front_kernel(x_ref, w_ref, b_ref, lng_ref, lnb_ref, ws_ref, bst_ref, o_ref, acc_ref, *, nk, tm):
    k = pl.program_id(1)

    @pl.when(k == 0)
    def _():
        acc_ref[...] = jnp.zeros_like(acc_ref)

    acc_ref[...] += jnp.dot(x_ref[...], w_ref[...], preferred_element_type=_F32)

    @pl.when(k == nk - 1)
    def _():
        row = lax.broadcasted_iota(jnp.int32, (CHUNK, CHUNK), 0)
        col = lax.broadcasted_iota(jnp.int32, (CHUNK, CHUNK), 1)
        causal = row >= col
        sqrt_half = np.float32(np.sqrt(0.5))

        def chunk(c, carry):
            r0 = pl.multiple_of(c * CHUNK, CHUNK)
            z = acc_ref[pl.ds(r0, CHUNK), :] + b_ref[...]
            z = 0.5 * z * (1.0 + lax.erf(z * sqrt_half))
            u = z[:, :A_WIDTH]
            v = _ln(z[:, A_WIDTH:], lng_ref[...], lnb_ref[...]).astype(_BF)
            for g in range(A_GROUPS):
                wg = jnp.where(causal, ws_ref[g], 0.0).astype(_BF)
                cs = slice(g * A_GROUP_CH, (g + 1) * A_GROUP_CH)
                vs = jnp.dot(wg, v[:, cs], preferred_element_type=_F32) + bst_ref[:, g:g + 1]
                o_ref[pl.ds(r0, CHUNK), cs] = (u[:, cs] * vs).astype(_BF)
            return carry

        lax.fori_loop(0, tm // CHUNK, chunk, 0)


def _gmlp_front(xb, w_in_b, b_in, ln_g, ln_b, w_s, b_s_t):
    n, d = xb.shape
    tm, tk = GF_TM, GF_TK
    nk = d // tk
    return pl.pallas_call(
        functools.partial(_gmlp_front_kernel, nk=nk, tm=tm),
        grid=(n // tm, nk),
        in_specs=[
            pl.BlockSpec((tm, tk), lambda i, k: (i, k)),
            pl.BlockSpec((tk, 2 * A_WIDTH), lambda i, k: (k, 0)),
            pl.BlockSpec((1, 2 * A_WIDTH), lambda i, k: (0, 0)),
            pl.BlockSpec((1, A_WIDTH), lambda i, k: (0, 0)),
            pl.BlockSpec((1, A_WIDTH), lambda i, k: (0, 0)),
            pl.BlockSpec((A_GROUPS, CHUNK, CHUNK), lambda i, k: (0, 0, 0)),
            pl.BlockSpec((CHUNK, A_GROUPS), lambda i, k: (0, 0)),
        ],
        out_specs=pl.BlockSpec((tm, A_WIDTH), lambda i, k: (i, 0)),
        out_shape=jax.ShapeDtypeStruct((n, A_WIDTH), _BF),
        scratch_shapes=[pltpu.VMEM((tm, 2 * A_WIDTH), _F32)],
        compiler_params=_params(("parallel", "arbitrary")),
        name="gmlp_front",
    )(xb, w_in_b, b_in, ln_g, ln_b, w_s, b_s_t)


def _proj_ln_kernel(*refs, has_bias):
    if has_bias:
        a_ref, w_ref, b_ref, xres_ref, g_ref, bb_ref, wr_ref, br_ref, h_ref, hb_ref, idx_ref, gate_ref = refs
    else:
        a_ref, w_ref, xres_ref, g_ref, bb_ref, wr_ref, br_ref, h_ref, hb_ref, idx_ref, gate_ref = refs
        b_ref = None
    mix = jnp.dot(a_ref[...], w_ref[...], preferred_element_type=_F32)
    if has_bias:
        mix = mix + b_ref[...]
    h = _ln(DEEPNORM_ALPHA * xres_ref[...] + mix, g_ref[...], bb_ref[...])
    h_ref[...] = h
    hb_ref[...] = h.astype(_BF)

    logits = jnp.dot(h, wr_ref[...], preferred_element_type=_F32,
                     precision=lax.Precision.HIGHEST) + br_ref[...]
    lane = lax.broadcasted_iota(jnp.int32, logits.shape, 1)
    cur = logits
    vals, idxs = [], []
    for _ in range(TOP_K):
        m = jnp.max(cur, axis=-1, keepdims=True)
        i = jnp.min(jnp.where(cur == m, lane, N_EXPERTS), axis=-1, keepdims=True)
        vals.append(m)
        idxs.append(i)
        cur = jnp.where(lane == i, -jnp.inf, cur)
    es = [jnp.exp(v - vals[0]) for v in vals]
    den = es[0] + es[1] + es[2] + es[3]
    colk = lax.broadcasted_iota(jnp.int32, idx_ref.shape, 1)
    idx_out = jnp.zeros(idx_ref.shape, jnp.int32)
    gate_out = jnp.zeros(gate_ref.shape, _F32)
    for k in range(TOP_K):
        idx_out = jnp.where(colk == k, idxs[k], idx_out)
        gate_out = jnp.where(colk == k, es[k] / den, gate_out)
    idx_ref[...] = idx_out
    gate_ref[...] = gate_out


def _proj_ln(a_b, w_b, bias, xres, ln_g, ln_b, w_r, b_r):
    n, d = xres.shape
    tm = PL_TM
    has_bias = bias is not None
    row = lambda i: (i, 0)
    const = lambda i: (0, 0)
    in_specs = [pl.BlockSpec((tm, a_b.shape[1]), row), pl.BlockSpec(w_b.shape, const)]
    args = [a_b, w_b]
    if has_bias:
        in_specs.append(pl.BlockSpec((1, d), const))
        args.append(bias)
    in_specs += [pl.BlockSpec((tm, d), row), pl.BlockSpec((1, d), const), pl.BlockSpec((1, d), const),
                 pl.BlockSpec((d, N_EXPERTS), const), pl.BlockSpec((1, N_EXPERTS), const)]
    args += [xres, ln_g, ln_b, w_r, b_r]
    return pl.pallas_call(
        functools.partial(_proj_ln_kernel, has_bias=has_bias),
        grid=(n // tm,),
        in_specs=in_specs,
        out_specs=[pl.BlockSpec((tm, d), row), pl.BlockSpec((tm, d), row),
                   pl.BlockSpec((tm, TOP_K), row), pl.BlockSpec((tm, TOP_K), row)],
        out_shape=[jax.ShapeDtypeStruct((n, d), _F32), jax.ShapeDtypeStruct((n, d), _BF),
                   jax.ShapeDtypeStruct((n, TOP_K), jnp.int32), jax.ShapeDtypeStruct((n, TOP_K), _F32)],
        compiler_params=_params(("parallel",)),
        name="proj_ln",
    )(*args)


def _moe_layout(n_tok):
    n_slots = n_tok * TOP_K + N_EXPERTS * MOE_SUB
    n_groups = N_EXPERTS + (n_tok * TOP_K) // MOE_RMAX
    return n_slots, n_groups


def _routing(idx):
    n = idx.shape[0]
    n_slots, n_groups = _moe_layout(n)
    e_ids = jnp.arange(N_EXPERTS, dtype=jnp.int32)
    onehot = (idx[:, :, None] == e_ids[None, None, :]).astype(jnp.int32).sum(axis=1)
    incl = jnp.cumsum(onehot, axis=0)
    counts = incl[-1]
    rank = jnp.take_along_axis(incl - onehot, idx, axis=1)
    padded = (counts + MOE_SUB - 1) // MOE_SUB * MOE_SUB
    pstart = jnp.cumsum(padded) - padded
    pos = (pstart[idx] + rank).astype(jnp.int32)
    tok = jnp.broadcast_to(jnp.arange(n, dtype=jnp.int32)[:, None], (n, TOP_K))
    slot_tok = jnp.zeros((n_slots,), jnp.int32).at[pos.reshape(-1)].set(tok.reshape(-1))
    ng = (counts + MOE_RMAX - 1) // MOE_RMAX
    gend = jnp.cumsum(ng)
    gstart = gend - ng
    total = gend[-1]
    gi = jnp.arange(n_groups, dtype=jnp.int32)
    gi_c = jnp.minimum(gi, total - 1)
    ge = jnp.searchsorted(gend, gi_c, side='right').astype(jnp.int32)
    local = gi_c - gstart[ge]
    gs = (pstart[ge] + local * MOE_RMAX).astype(jnp.int32)
    rows = jnp.clip(counts[ge] - local * MOE_RMAX, 0, MOE_RMAX)
    nsub = jnp.where(gi < total, (rows + MOE_SUB - 1) // MOE_SUB, 0).astype(jnp.int32)
    tail = jnp.sum(padded).astype(jnp.int32).reshape(1)
    return pos, slot_tok, ge, gs, nsub, tail


def _moe_kernel(ge_ref, gs_ref, nsub_ref, tail_ref, tok_ref,
                h_hbm, wg_ref, wu_ref, wd_ref, bg_ref, bu_ref, bd_ref,
                y_hbm,
                xg, xb, yacc, wgb, wub, wdb, sem_g, sem_o):
    g = pl.program_id(0)
    j = pl.program_id(1)
    ns = nsub_ref[g]

    def gather_copy(t, i):
        return pltpu.make_async_copy(h_hbm.at[pl.ds(t, 1)], xg.at[pl.ds(i, 1)], sem_g)

    def out_copy(base, r0):
        dst = pl.multiple_of(base + r0, MOE_SUB)
        return pltpu.make_async_copy(yacc.at[pl.ds(r0, MOE_SUB)], y_hbm.at[pl.ds(dst, MOE_SUB)], sem_o)

    @pl.when(ns > 0)
    def _():
        base = pl.multiple_of(gs_ref[g], MOE_SUB)

        @pl.when(j == 0)
        def _():
            def issue(i, c):
                gather_copy(tok_ref[base + i], i).start()
                return c
            lax.fori_loop(0, ns * MOE_SUB, issue, 0)

            def wait_row(i, c):
                gather_copy(0, 0).wait()
                return c

            def wait_sub(s, c):
                return lax.fori_loop(0, MOE_SUB, wait_row, c, unroll=8)
            lax.fori_loop(0, ns, wait_sub, 0)

            def cast(s, c):
                r0 = pl.multiple_of(s * MOE_SUB, MOE_SUB)
                xb[pl.ds(r0, MOE_SUB), :] = xg[pl.ds(r0, MOE_SUB), :].astype(_BF)
                return c
            lax.fori_loop(0, ns, cast, 0)

        wgb[...] = wg_ref[0].astype(_BF)
        wub[...] = wu_ref[0].astype(_BF)
        wdb[...] = wd_ref[0].astype(_BF)

        def sub(s, c):
            r0 = pl.multiple_of(s * MOE_SUB, MOE_SUB)
            xs = xb[pl.ds(r0, MOE_SUB), :]
            gg = jnp.dot(xs, wgb[...], preferred_element_type=_F32) + bg_ref[0]
            uu = jnp.dot(xs, wub[...], preferred_element_type=_F32) + bu_ref[0]
            gg = jnp.minimum(gg, SWIGLU_LIMIT)
            uu = jnp.clip(uu, -SWIGLU_LIMIT, SWIGLU_LIMIT)
            act = (uu + 1.0) * (gg * jax.nn.sigmoid(SWIGLU_ALPHA * gg))
            contrib = jnp.dot(act.astype(_BF), wdb[...], preferred_element_type=_F32)

            @pl.when(j == 0)
            def _():
                yacc[pl.ds(r0, MOE_SUB), :] = contrib + bd_ref[0]

            @pl.when(j > 0)
            def _():
                yacc[pl.ds(r0, MOE_SUB), :] += contrib

            @pl.when(j == MOE_J - 1)
            def _():
                out_copy(base, r0).start()
            return c
        lax.fori_loop(0, ns, sub, 0)

        @pl.when(j == MOE_J - 1)
        def _():
            def drain(s, c):
                out_copy(base, pl.multiple_of(s * MOE_SUB, MOE_SUB)).wait()
                return c
            lax.fori_loop(0, ns, drain, 0)

    @pl.when((g == pl.num_programs(0) - 1) & (j == MOE_J - 1))
    def _():
        tail = pl.multiple_of(tail_ref[0], MOE_SUB)
        n_tail = (y_hbm.shape[0] - tail) // MOE_SUB
        yacc[pl.ds(0, MOE_SUB), :] = jnp.zeros((MOE_SUB, yacc.shape[1]), _F32)

        def tail_copy(s):
            dst = pl.multiple_of(tail + s * MOE_SUB, MOE_SUB)
            return pltpu.make_async_copy(yacc.at[pl.ds(0, MOE_SUB)], y_hbm.at[pl.ds(dst, MOE_SUB)], sem_o)

        def start(s, c):
            tail_copy(s).start()
            return c
        lax.fori_loop(0, n_tail, start, 0)

        def done(s, c):
            tail_copy(s).wait()
            return c
        lax.fori_loop(0, n_tail, done, 0)


def _moe_experts(h, w_gu, b_gu, w_down, b_down, slot_tok, ge, gs, nsub, tail):
    n, d = h.shape
    n_slots, n_groups = _moe_layout(n)
    f, jn = MOE_F, MOE_J

    def jj(g, j, nsub_ref):
        return jnp.where(nsub_ref[g] > 0, j, jn - 1)

    grid_spec = pltpu.PrefetchScalarGridSpec(
        num_scalar_prefetch=5,
        grid=(n_groups, jn),
        in_specs=[
            pl.BlockSpec(memory_space=pl.ANY),
            pl.BlockSpec((1, d, f), lambda g, j, ge, gs, ns, tl, tk: (ge[g], 0, jj(g, j, ns))),
            pl.BlockSpec((1, d, f), lambda g, j, ge, gs, ns, tl, tk: (ge[g], 0, jn + jj(g, j, ns))),
            pl.BlockSpec((1, f, d), lambda g, j, ge, gs, ns, tl, tk: (ge[g], jj(g, j, ns), 0)),
            pl.BlockSpec((1, 1, f), lambda g, j, ge, gs, ns, tl, tk: (ge[g], 0, jj(g, j, ns))),
            pl.BlockSpec((1, 1, f), lambda g, j, ge, gs, ns, tl, tk: (ge[g], 0, jn + jj(g, j, ns))),
            pl.BlockSpec((1, 1, d), lambda g, j, ge, gs, ns, tl, tk: (ge[g], 0, 0)),
        ],
        out_specs=pl.BlockSpec(memory_space=pl.ANY),
        scratch_shapes=[
            pltpu.VMEM((MOE_RMAX, d), _F32),
            pltpu.VMEM((MOE_RMAX, d), _BF),
            pltpu.VMEM((MOE_RMAX, d), _F32),
            pltpu.VMEM((d, f), _BF),
            pltpu.VMEM((d, f), _BF),
            pltpu.VMEM((f, d), _BF),
            pltpu.SemaphoreType.DMA(()),
            pltpu.SemaphoreType.DMA(()),
        ],
    )
    return pl.pallas_call(
        _moe_kernel,
        grid_spec=grid_spec,
        out_shape=jax.ShapeDtypeStruct((n_slots, d), _F32),
        compiler_params=_params(("arbitrary", "arbitrary")),
        name="moe_experts",
    )(ge, gs, nsub, tail, slot_tok, h, w_gu, w_gu, w_down,
      b_gu.reshape(N_EXPERTS, 1, 2 * EXPERT_FF), b_gu.reshape(N_EXPERTS, 1, 2 * EXPERT_FF),
      b_down.reshape(N_EXPERTS, 1, d))


def _combine_ln2_kernel(pos_ref, h_ref, hb_ref, p_ref, gate_ref, wg_ref, bg_ref, wp_ref, g_ref, b_ref, y_hbm,
                        x_ref, xb_ref, yg, sem, *, tm):
    i = pl.program_id(0)
    base = i * (tm * TOP_K)

    def issue(r, c):
        for k in range(TOP_K):
            s = pos_ref[base + r * TOP_K + k]
            pltpu.make_async_copy(y_hbm.at[pl.ds(s, 1)], yg.at[k, pl.ds(r, 1)], sem).start()
        return c
    lax.fori_loop(0, tm, issue, 0)

    gate_v = jax.nn.sigmoid(jnp.dot(hb_ref[...], wg_ref[...], preferred_element_type=_F32) + bg_ref[...])
    pp = jnp.dot(p_ref[...].astype(_BF), wp_ref[...], preferred_element_type=_F32)
    acc = DEEPNORM_ALPHA * h_ref[...] + gate_v * pp

    def wait_row(r, c):
        pltpu.make_async_copy(y_hbm.at[pl.ds(0, 1)], yg.at[0, pl.ds(0, 1)], sem).wait()
        return c
    lax.fori_loop(0, tm * TOP_K, wait_row, 0, unroll=8)

    gates = gate_ref[...]
    for k in range(TOP_K):
        acc = acc + gates[:, k:k + 1] * yg[k]
    xn = _ln(acc, g_ref[...], b_ref[...])
    x_ref[...] = xn
    xb_ref[...] = xn.astype(_BF)


def _combine_ln2(h, hb, p_i, gates, pos, y_sorted, w_g_b, b_g, w_p_b, ln_g, ln_b):
    n, d = h.shape
    tm = C2_TM
    row = lambda i, pos_ref: (i, 0)
    const = lambda i, pos_ref: (0, 0)
    grid_spec = pltpu.PrefetchScalarGridSpec(
        num_scalar_prefetch=1,
        grid=(n // tm,),
        in_specs=[
            pl.BlockSpec((tm, d), row), pl.BlockSpec((tm, d), row), pl.BlockSpec((tm, PLE_DIM), row),
            pl.BlockSpec((tm, TOP_K), row),
            pl.BlockSpec((d, d), const), pl.BlockSpec((1, d), const), pl.BlockSpec((PLE_DIM, d), const),
            pl.BlockSpec((1, d), const), pl.BlockSpec((1, d), const),
            pl.BlockSpec(memory_space=pl.ANY),
        ],
        out_specs=[pl.BlockSpec((tm, d), row), pl.BlockSpec((tm, d), row)],
        scratch_shapes=[pltpu.VMEM((TOP_K, tm, d), _F32), pltpu.SemaphoreType.DMA(())],
    )
    return pl.pallas_call(
        functools.partial(_combine_ln2_kernel, tm=tm),
        grid_spec=grid_spec,
        out_shape=[jax.ShapeDtypeStruct((n, d), _F32), jax.ShapeDtypeStruct((n, d), _BF)],
        compiler_params=_params(("arbitrary",)),
        name="combine_ln2",
    )(pos.reshape(-1), h, hb, p_i, gates, w_g_b, b_g, w_p_b, ln_g, ln_b, y_sorted)


def _mm_kernel(a_ref, w_ref, o_ref):
    o_ref[...] = jnp.dot(a_ref[...], w_ref[...], preferred_element_type=_F32)


def _mm(a_b, w_b):
    n, d = a_b.shape
    nout = w_b.shape[1]
    tm, tn = MM_TM, MM_TN
    return pl.pallas_call(
        _mm_kernel,
        grid=(n // tm, nout // tn),
        in_specs=[pl.BlockSpec((tm, d), lambda i, j: (i, 0)), pl.BlockSpec((d, tn), lambda i, j: (0, j))],
        out_specs=pl.BlockSpec((tm, tn), lambda i, j: (i, j)),
        out_shape=jax.ShapeDtypeStruct((n, nout), _F32),
        compiler_params=_params(("parallel", "arbitrary")),
        name="mm",
    )(a_b, w_b)


def _attn_kernel(slopes_ref, q0_ref, q1_ref, q2_ref, kc_ref, kp_ref, vc_ref, vp_ref, o_ref,
                 bias_s, m_s, l_s, acc_s):
    j = pl.program_id(1)
    h = pl.program_id(2)
    neg_slope = -slopes_ref[h]
    qi = lax.broadcasted_iota(jnp.int32, (QBLK, 2 * QBLK), 0)
    ki = lax.broadcasted_iota(jnp.int32, (QBLK, 2 * QBLK), 1)
    dist = qi + QBLK - ki
    n_groups = B_GROUPS
    for g, d in enumerate(B_DILATIONS):
        n_back = B_WINDOWS[g] // d
        valid = (dist >= 0) & (dist <= n_back)
        bias = jnp.where(valid, neg_slope * (dist * d).astype(_F32), NEG_INF)
        bias_s[g] = bias
        bias_s[n_groups + g] = jnp.where(j == 0, jnp.where(ki < QBLK, NEG_INF, bias), bias)

    scale = np.float32(HEAD_DIM ** -0.5)

    def block(g, d, q_ref, start, kband, vband, bias):
        q = q_ref[pl.ds(start, QBLK, stride=d), :].astype(_BF) if d > 1 else q_ref[pl.ds(start, QBLK), :].astype(_BF)
        s = lax.dot_general(q, kband, (((1,), (1,)), ((), ())), preferred_element_type=_F32)
        s = s * scale + bias
        m = jnp.max(s, axis=-1, keepdims=True)
        p = jnp.exp(s - m)
        l = jnp.sum(p, axis=-1, keepdims=True)
        pv = jnp.dot(p.astype(_BF), vband, preferred_element_type=_F32)
        mb = jnp.broadcast_to(m, (QBLK, HEAD_DIM))
        lb = jnp.broadcast_to(l, (QBLK, HEAD_DIM))
        if d > 1:
            idx = (g, pl.ds(start, QBLK, stride=d), slice(None))
        else:
            idx = (g, pl.ds(start, QBLK), slice(None))
        m_s[idx] = mb
        l_s[idx] = lb
        acc_s[idx] = pv

    def rows(ref, start, size, d):
        if d > 1:
            return ref[pl.ds(start, size, stride=d), :].astype(_BF)
        return ref[pl.ds(start, size), :].astype(_BF)

    for g, d in enumerate(B_DILATIONS):
        q_ref = (q0_ref, q1_ref, q2_ref)[g]
        span = QBLK * d
        nloc = ATT_T // span

        def first_block(r, g=g, d=d, q_ref=q_ref, span=span, nloc=nloc):
            pstart = (nloc - 1) * span + r
            kband = jnp.concatenate([rows(kp_ref, pstart, QBLK, d), rows(kc_ref, r, QBLK, d)], axis=0)
            vband = jnp.concatenate([rows(vp_ref, pstart, QBLK, d), rows(vc_ref, r, QBLK, d)], axis=0)
            block(g, d, q_ref, r, kband, vband, bias_s[n_groups + g])

        def later_block(r, n, g=g, d=d, q_ref=q_ref, span=span):
            bstart = (n - 1) * span + r
            kband = rows(kc_ref, bstart, 2 * QBLK, d)
            vband = rows(vc_ref, bstart, 2 * QBLK, d)
            block(g, d, q_ref, n * span + r, kband, vband, bias_s[g])

        if nloc == 1:
            def body(r, c):
                first_block(r)
                return c
            lax.fori_loop(0, d, body, 0)
        else:
            for r in range(d):
                first_block(r)

                def body(n, c, r=r):
                    later_block(r, n)
                    return c
                lax.fori_loop(1, nloc, body, 0)

    m0, m1, m2 = m_s[0], m_s[1], m_s[2]
    mm = jnp.maximum(jnp.maximum(m0, m1), m2)
    w0 = jnp.exp(m0 - mm)
    w1 = jnp.exp(m1 - mm)
    w2 = jnp.exp(m2 - mm)
    num = w0 * acc_s[0] + w1 * acc_s[1] + w2 * acc_s[2]
    den = w0 * l_s[0] + w1 * l_s[1] + w2 * l_s[2]
    o_ref[...] = (num / den).astype(o_ref.dtype)


def _dilated_attention(q, kv, bsz, seq):
    n = q.shape[0]
    t = ATT_T
    tiles = seq // t
    hd = B_HEADS
    slopes = 2.0 ** (-8.0 * jnp.arange(1, hd + 1, dtype=_F32) / hd)

    def qspec(g):
        return pl.BlockSpec((t, HEAD_DIM), lambda b, j, h, s, g=g: (b * tiles + j, g * hd + h))

    def cur(off):
        return pl.BlockSpec((t, HEAD_DIM), lambda b, j, h, s: (b * tiles + j, off + h))

    def prev(off):
        return pl.BlockSpec((t, HEAD_DIM), lambda b, j, h, s: (b * tiles + jnp.maximum(j - 1, 0), off + h))

    grid_spec = pltpu.PrefetchScalarGridSpec(
        num_scalar_prefetch=1,
        grid=(bsz, tiles, hd),
        in_specs=[qspec(0), qspec(1), qspec(2), cur(0), prev(0), cur(hd), prev(hd)],
        out_specs=pl.BlockSpec((t, HEAD_DIM), lambda b, j, h, s: (b * tiles + j, h)),
        scratch_shapes=[
            pltpu.VMEM((2 * B_GROUPS, QBLK, 2 * QBLK), _F32),
            pltpu.VMEM((B_GROUPS, t, HEAD_DIM), _F32),
            pltpu.VMEM((B_GROUPS, t, HEAD_DIM), _F32),
            pltpu.VMEM((B_GROUPS, t, HEAD_DIM), _F32),
        ],
    )
    return pl.pallas_call(
        _attn_kernel,
        grid_spec=grid_spec,
        out_shape=jax.ShapeDtypeStruct((n, hd * HEAD_DIM), _BF),
        compiler_params=_params(("parallel", "parallel", "arbitrary")),
        name="dilated_attn",
    )(slopes, q, q, q, kv, kv, kv, kv)


def kernel(x, p, a_w_in, a_b_in, a_ln_g, a_ln_b, a_w_s, a_b_s, a_w_out, a_b_out, kv_w, b_w_q, b_w_o, ln1_g, ln1_b, ln2_g, ln2_b, router_w, router_b, moe_w_gu, moe_b_gu, moe_w_down, moe_b_down, ple_w_p, ple_w_g, ple_b_g):
    bsz, seq, d = x.shape
    n = bsz * seq
    xf = x.reshape(n, d)
    xb = xf.astype(_BF)
    row = lambda v: v.reshape(1, -1)
    kv = None
    for i in range(DEPTH):
        if i < N_A_LAYERS:
            gated = _gmlp_front(xb, a_w_in[i].astype(_BF), row(a_b_in[i]), row(a_ln_g[i]), row(a_ln_b[i]),
                                a_w_s[i], a_b_s[i].T)
            h, hb, idx, gates = _proj_ln(gated, a_w_out[i].astype(_BF), row(a_b_out[i]), xf,
                                         row(ln1_g[i]), row(ln1_b[i]), router_w[i], row(router_b[i]))
        else:
            jl = i - N_A_LAYERS
            q = _mm(xb, b_w_q[jl].astype(_BF))
            att = _dilated_attention(q, kv, bsz, seq)
            h, hb, idx, gates = _proj_ln(att, b_w_o[jl].astype(_BF), None, xf,
                                         row(ln1_g[i]), row(ln1_b[i]), router_w[i], row(router_b[i]))
        pos, slot_tok, ge, gs, nsub, tail = _routing(idx)
        y_sorted = _moe_experts(h, moe_w_gu[i], moe_b_gu[i], moe_w_down[i], moe_b_down[i],
                                slot_tok, ge, gs, nsub, tail)
        xf, xb = _combine_ln2(h, hb, p[i].reshape(n, PLE_DIM), gates, pos, y_sorted,
                              ple_w_g[i].astype(_BF), row(ple_b_g[i]), ple_w_p[i].astype(_BF),
                              row(ln2_g[i]), row(ln2_b[i]))
        if i == N_A_LAYERS - 1:
            kv = _mm(xb, kv_w.astype(_BF))
    return xf.reshape(bsz, seq, d)
```

```python
import functools

import jax
import jax.numpy as jnp
import numpy as np
from jax import lax
from jax.experimental import pallas as pl
from jax.experimental.pallas import tpu as pltpu

D_MODEL = 2048
DEPTH = 4
N_A_LAYERS = DEPTH // 2
CHUNK = 128
A_WIDTH = D_MODEL
A_GROUPS = 8
A_GROUP_CH = A_WIDTH // A_GROUPS
HEAD_DIM = 128
B_HEADS = D_MODEL // HEAD_DIM
B_WINDOWS = (128, 512, 2048)
B_DILATIONS = (1, 4, 16)
B_GROUPS = len(B_WINDOWS)
QBLK = 128
N_EXPERTS = 32
TOP_K = 4
EXPERT_FF = D_MODEL // 2
SWIGLU_LIMIT = 7.0
SWIGLU_ALPHA = 1.702
PLE_DIM = 256
LN_EPS = 1e-5
DEEPNORM_ALPHA = (2.0 * DEPTH) ** 0.25
NEG_INF = -1e30

V7X_VMEM_BYTES = 64 * 1024 * 1024
VMEM_LIMIT = 56 * 1024 * 1024
GF_TM, GF_TK = 512, 512
PL_TM = 256
MM_TM, MM_TN = 1024, 512
ATT_T = QBLK * max(B_DILATIONS)
MOE_F = 256
MOE_J = EXPERT_FF // MOE_F
MOE_SUB = 256
MOE_RMAX = 1280
C2_TM = 256

_BF = jnp.bfloat16
_F32 = jnp.float32


def _params(sem):
    return pltpu.CompilerParams(dimension_semantics=sem, vmem_limit_bytes=VMEM_LIMIT)


def _ln(y, g, b):
    mu = jnp.mean(y, axis=-1, keepdims=True)
    yc = y - mu
    var = jnp.mean(yc * yc, axis=-1, keepdims=True)
    return yc * lax.rsqrt(var + LN_EPS) * g + b


def _gmlp_front_kernel(x_ref, w_ref, b_ref, lng_ref, lnb_ref, ws_ref, bst_ref, o_ref, acc_ref, *, nk, tm):
    k = pl.program_id(1)

    @pl.when(k == 0)
    def _():
        acc_ref[...] = jnp.zeros_like(acc_ref)

    acc_ref[...] += jnp.dot(x_ref[...], w_ref[...], preferred_element_type=_F32)

    @pl.when(k == nk - 1)
    def _():
        row = lax.broadcasted_iota(jnp.int32, (CHUNK, CHUNK), 0)
        col = lax.broadcasted_iota(jnp.int32, (CHUNK, CHUNK), 1)
        causal = row >= col
        sqrt_half = np.float32(np.sqrt(0.5))

        def chunk(c, carry):
            r0 = pl.multiple_of(c * CHUNK, CHUNK)
            z = acc_ref[pl.ds(r0, CHUNK), :] + b_ref[...]
            z = 0.5 * z * (1.0 + lax.erf(z * sqrt_half))
            u = z[:, :A_WIDTH]
            v = _ln(z[:, A_WIDTH:], lng_ref[...], lnb_ref[...]).astype(_BF)
            for g in range(A_GROUPS):
                wg = jnp.where(causal, ws_ref[g], 0.0).astype(_BF)
                cs = slice(g * A_GROUP_CH, (g + 1) * A_GROUP_CH)
                vs = jnp.dot(wg, v[:, cs], preferred_element_type=_F32) + bst_ref[:, g:g + 1]
                o_ref[pl.ds(r0, CHUNK), cs] = (u[:, cs] * vs).astype(_BF)
            return carry

        lax.fori_loop(0, tm // CHUNK, chunk, 0)


def _gmlp_front(xb, w_in_b, b_in, ln_g, ln_b, w_s, b_s_t):
    n, d = xb.shape
    tm, tk = GF_TM, GF_TK
    nk = d // tk
    return pl.pallas_call(
        functools.partial(_gmlp_front_kernel, nk=nk, tm=tm),
        grid=(n // tm, nk),
        in_specs=[
            pl.BlockSpec((tm, tk), lambda i, k: (i, k)),
            pl.BlockSpec((tk, 2 * A_WIDTH), lambda i, k: (k, 0)),
            pl.BlockSpec((1, 2 * A_WIDTH), lambda i, k: (0, 0)),
            pl.BlockSpec((1, A_WIDTH), lambda i, k: (0, 0)),
            pl.BlockSpec((1, A_WIDTH), lambda i, k: (0, 0)),
            pl.BlockSpec((A_GROUPS, CHUNK, CHUNK), lambda i, k: (0, 0, 0)),
            pl.BlockSpec((CHUNK, A_GROUPS), lambda i, k: (0, 0)),
        ],
        out_specs=pl.BlockSpec((tm, A_WIDTH), lambda i, k: (i, 0)),
        out_shape=jax.ShapeDtypeStruct((n, A_WIDTH), _BF),
        scratch_shapes=[pltpu.VMEM((tm, 2 * A_WIDTH), _F32)],
        compiler_params=_params(("parallel", "arbitrary")),
        name="gmlp_front",
    )(xb, w_in_b, b_in, ln_g, ln_b, w_s, b_s_t)


def _proj_ln_kernel(*refs, has_bias):
    if has_bias:
        a_ref, w_ref, b_ref, xres_ref, g_ref, bb_ref, wr_ref, br_ref, h_ref, hb_ref, idx_ref, gate_ref = refs
    else:
        a_ref, w_ref, xres_ref, g_ref, bb_ref, wr_ref, br_ref, h_ref, hb_ref, idx_ref, gate_ref = refs
        b_ref = None
    mix = jnp.dot(a_ref[...], w_ref[...], preferred_element_type=_F32)
    if has_bias:
        mix = mix + b_ref[...]
    h = _ln(DEEPNORM_ALPHA * xres_ref[...] + mix, g_ref[...], bb_ref[...])
    h_ref[...] = h
    h_hi = h.astype(_BF)
    hb_ref[...] = h_hi

    h_lo = (h - h_hi.astype(_F32)).astype(_BF)
    wr = wr_ref[...]
    w_hi = wr.astype(_BF)
    w_lo = (wr - w_hi.astype(_F32)).astype(_BF)
    logits = (jnp.dot(h_hi, w_hi, preferred_element_type=_F32)
              + (jnp.dot(h_hi, w_lo, preferred_element_type=_F32)
                 + jnp.dot(h_lo, w_hi, preferred_element_type=_F32))) + br_ref[...]
    lane = lax.broadcasted_iota(jnp.int32, logits.shape, 1)
    cur = logits
    vals, idxs = [], []
    for _ in range(TOP_K):
        m = jnp.max(cur, axis=-1, keepdims=True)
        i = jnp.min(jnp.where(cur == m, lane, N_EXPERTS), axis=-1, keepdims=True)
        vals.append(m)
        idxs.append(i)
        cur = jnp.where(lane == i, -jnp.inf, cur)
    es = [jnp.exp(v - vals[0]) for v in vals]
    den = es[0] + es[1] + es[2] + es[3]
    colk = lax.broadcasted_iota(jnp.int32, idx_ref.shape, 1)
    idx_out = jnp.zeros(idx_ref.shape, jnp.int32)
    gate_out = jnp.zeros(gate_ref.shape, _F32)
    for k in range(TOP_K):
        idx_out = jnp.where(colk == k, idxs[k], idx_out)
        gate_out = jnp.where(colk == k, es[k] / den, gate_out)
    idx_ref[...] = idx_out
    gate_ref[...] = gate_out


def _proj_ln(a_b, w_b, bias, xres, ln_g, ln_b, w_r, b_r):
    n, d = xres.shape
    tm = PL_TM
    has_bias = bias is not None
    row = lambda i: (i, 0)
    const = lambda i: (0, 0)
    in_specs = [pl.BlockSpec((tm, a_b.shape[1]), row), pl.BlockSpec(w_b.shape, const)]
    args = [a_b, w_b]
    if has_bias:
        in_specs.append(pl.BlockSpec((1, d), const))
        args.append(bias)
    in_specs += [pl.BlockSpec((tm, d), row), pl.BlockSpec((1, d), const), pl.BlockSpec((1, d), const),
                 pl.BlockSpec((d, N_EXPERTS), const), pl.BlockSpec((1, N_EXPERTS), const)]
    args += [xres, ln_g, ln_b, w_r, b_r]
    return pl.pallas_call(
        functools.partial(_proj_ln_kernel, has_bias=has_bias),
        grid=(n // tm,),
        in_specs=in_specs,
        out_specs=[pl.BlockSpec((tm, d), row), pl.BlockSpec((tm, d), row),
                   pl.BlockSpec((tm, TOP_K), row), pl.BlockSpec((tm, TOP_K), row)],
        out_shape=[jax.ShapeDtypeStruct((n, d), _F32), jax.ShapeDtypeStruct((n, d), _BF),
                   jax.ShapeDtypeStruct((n, TOP_K), jnp.int32), jax.ShapeDtypeStruct((n, TOP_K), _F32)],
        compiler_params=_params(("parallel",)),
        name="proj_ln",
    )(*args)


def _moe_layout(n_tok):
    n_slots = n_tok * TOP_K + N_EXPERTS * MOE_SUB
    n_groups = N_EXPERTS + (n_tok * TOP_K) // MOE_RMAX
    return n_slots, n_groups


def _routing(idx):
    n = idx.shape[0]
    n_slots, n_groups = _moe_layout(n)
    e_ids = jnp.arange(N_EXPERTS, dtype=jnp.int32)
    onehot = (idx[:, :, None] == e_ids[None, None, :]).astype(jnp.int32).sum(axis=1)
    incl = jnp.cumsum(onehot, axis=0)
    counts = incl[-1]
    rank = jnp.take_along_axis(incl - onehot, idx, axis=1)
    padded = (counts + MOE_SUB - 1) // MOE_SUB * MOE_SUB
    pstart = jnp.cumsum(padded) - padded
    pos = (pstart[idx] + rank).astype(jnp.int32)
    tok = jnp.broadcast_to(jnp.arange(n, dtype=jnp.int32)[:, None], (n, TOP_K))
    slot_tok = jnp.zeros((n_slots,), jnp.int32).at[pos.reshape(-1)].set(tok.reshape(-1))
    ng = (counts + MOE_RMAX - 1) // MOE_RMAX
    gend = jnp.cumsum(ng)
    gstart = gend - ng
    total = gend[-1]
    gi = jnp.arange(n_groups, dtype=jnp.int32)
    gi_c = jnp.minimum(gi, total - 1)
    ge = jnp.searchsorted(gend, gi_c, side='right').astype(jnp.int32)
    local = gi_c - gstart[ge]
    gs = (pstart[ge] + local * MOE_RMAX).astype(jnp.int32)
    rows = jnp.clip(counts[ge] - local * MOE_RMAX, 0, MOE_RMAX)
    nsub = jnp.where(gi < total, (rows + MOE_SUB - 1) // MOE_SUB, 0).astype(jnp.int32)
    tail = jnp.sum(padded).astype(jnp.int32).reshape(1)
    return pos, slot_tok, ge, gs, nsub, tail


def _moe_kernel(ge_ref, gs_ref, nsub_ref, tail_ref, tok_ref,
                h_hbm, wg_ref, wu_ref, wd_ref, bg_ref, bu_ref, bd_ref,
                y_hbm,
                xg, xb, yacc, wgb, wub, wdb, sem_g, sem_o):
    g = pl.program_id(0)
    j = pl.program_id(1)
    ns = nsub_ref[g]

    def gather_copy(t, i8, u):
        return pltpu.make_async_copy(h_hbm.at[pl.ds(t, 1)], xg.at[i8, pl.ds(u, 1)], sem_g)

    def out_copy(base, r0):
        dst = pl.multiple_of(base + r0, MOE_SUB)
        return pltpu.make_async_copy(yacc.at[pl.ds(r0, MOE_SUB)], y_hbm.at[pl.ds(dst, MOE_SUB)], sem_o)

    @pl.when(ns > 0)
    def _():
        base = pl.multiple_of(gs_ref[g], MOE_SUB)

        @pl.when(j == 0)
        def _():
            def issue_sub(s, c):
                r0 = pl.multiple_of(s * MOE_SUB, MOE_SUB)

                def issue(i8, c2):
                    t8 = s * (MOE_SUB // 8) + i8
                    for u in range(8):
                        gather_copy(tok_ref[base + t8 * 8 + u], t8, u).start()
                    return c2
                return lax.fori_loop(0, MOE_SUB // 8, issue, c)
            lax.fori_loop(0, ns, issue_sub, 0)

            def wait_row(i, c):
                gather_copy(0, 0, 0).wait()
                return c

            def wait_sub(s, c):
                return lax.fori_loop(0, MOE_SUB, wait_row, c, unroll=8)
            lax.fori_loop(0, ns, wait_sub, 0)

            def cast(s, c):
                r0 = pl.multiple_of(s * MOE_SUB, MOE_SUB)
                t0 = s * (MOE_SUB // 8)
                xs = xg[pl.ds(t0, MOE_SUB // 8)].reshape(MOE_SUB, xb.shape[1])
                xb[pl.ds(r0, MOE_SUB), :] = xs.astype(_BF)
                return c
            lax.fori_loop(0, ns, cast, 0)

        wgb[...] = wg_ref[0].astype(_BF)
        wub[...] = wu_ref[0].astype(_BF)
        wdb[...] = wd_ref[0].astype(_BF)

        def sub(s, c):
            r0 = pl.multiple_of(s * MOE_SUB, MOE_SUB)
            xs = xb[pl.ds(r0, MOE_SUB), :]
            gg = jnp.dot(xs, wgb[...], preferred_element_type=_F32) + bg_ref[0]
            uu = jnp.dot(xs, wub[...], preferred_element_type=_F32) + bu_ref[0]
            gg = jnp.minimum(gg, SWIGLU_LIMIT)
            uu = jnp.clip(uu, -SWIGLU_LIMIT, SWIGLU_LIMIT)
            act = (uu + 1.0) * (gg * jax.nn.sigmoid(SWIGLU_ALPHA * gg))
            contrib = jnp.dot(act.astype(_BF), wdb[...], preferred_element_type=_F32)

            @pl.when(j == 0)
            def _():
                yacc[pl.ds(r0, MOE_SUB), :] = contrib + bd_ref[0]

            @pl.when(j > 0)
            def _():
                yacc[pl.ds(r0, MOE_SUB), :] += contrib

            @pl.when(j == MOE_J - 1)
            def _():
                out_copy(base, r0).start()
            return c
        lax.fori_loop(0, ns, sub, 0)

        @pl.when(j == MOE_J - 1)
        def _():
            def drain(s, c):
                out_copy(base, pl.multiple_of(s * MOE_SUB, MOE_SUB)).wait()
                return c
            lax.fori_loop(0, ns, drain, 0)

    @pl.when((g == pl.num_programs(0) - 1) & (j == MOE_J - 1))
    def _():
        tail = pl.multiple_of(tail_ref[0], MOE_SUB)
        n_tail = (y_hbm.shape[0] - tail) // MOE_SUB
        yacc[pl.ds(0, MOE_SUB), :] = jnp.zeros((MOE_SUB, yacc.shape[1]), _F32)

        def tail_copy(s):
            dst = pl.multiple_of(tail + s * MOE_SUB, MOE_SUB)
            return pltpu.make_async_copy(yacc.at[pl.ds(0, MOE_SUB)], y_hbm.at[pl.ds(dst, MOE_SUB)], sem_o)

        def start(s, c):
            tail_copy(s).start()
            return c
        lax.fori_loop(0, n_tail, start, 0)

        def done(s, c):
            tail_copy(s).wait()
            return c
        lax.fori_loop(0, n_tail, done, 0)


def _moe_experts(h, w_gu, b_gu, w_down, b_down, layer, slot_tok, ge, gs, nsub, tail):
    n, d = h.shape
    n_rows = w_gu.shape[0] * N_EXPERTS
    w_gu = w_gu.reshape(n_rows, d, 2 * EXPERT_FF)
    w_down = w_down.reshape(n_rows, EXPERT_FF, d)
    b_gu = b_gu.reshape(n_rows, 1, 2 * EXPERT_FF)
    b_down = b_down.reshape(n_rows, 1, d)
    ge = ge + layer * N_EXPERTS
    n_slots, n_groups = _moe_layout(n)
    f, jn = MOE_F, MOE_J

    def jj(g, j, nsub_ref):
        return jnp.where(nsub_ref[g] > 0, j, jn - 1)

    grid_spec = pltpu.PrefetchScalarGridSpec(
        num_scalar_prefetch=5,
        grid=(n_groups, jn),
        in_specs=[
            pl.BlockSpec(memory_space=pl.ANY),
            pl.BlockSpec((1, d, f), lambda g, j, ge, gs, ns, tl, tk: (ge[g], 0, jj(g, j, ns))),
            pl.BlockSpec((1, d, f), lambda g, j, ge, gs, ns, tl, tk: (ge[g], 0, jn + jj(g, j, ns))),
            pl.BlockSpec((1, f, d), lambda g, j, ge, gs, ns, tl, tk: (ge[g], jj(g, j, ns), 0)),
            pl.BlockSpec((1, 1, f), lambda g, j, ge, gs, ns, tl, tk: (ge[g], 0, jj(g, j, ns))),
            pl.BlockSpec((1, 1, f), lambda g, j, ge, gs, ns, tl, tk: (ge[g], 0, jn + jj(g, j, ns))),
            pl.BlockSpec((1, 1, d), lambda g, j, ge, gs, ns, tl, tk: (ge[g], 0, 0)),
        ],
        out_specs=pl.BlockSpec(memory_space=pl.ANY),
        scratch_shapes=[
            pltpu.VMEM((MOE_RMAX // 8, 8, d), _F32),
            pltpu.VMEM((MOE_RMAX, d), _BF),
            pltpu.VMEM((MOE_RMAX, d), _F32),
            pltpu.VMEM((d, f), _BF),
            pltpu.VMEM((d, f), _BF),
            pltpu.VMEM((f, d), _BF),
            pltpu.SemaphoreType.DMA(()),
            pltpu.SemaphoreType.DMA(()),
        ],
    )
    return pl.pallas_call(
        _moe_kernel,
        grid_spec=grid_spec,
        out_shape=jax.ShapeDtypeStruct((n_slots, d), _F32),
        compiler_params=_params(("arbitrary", "arbitrary")),
        name="moe_experts",
    )(ge, gs, nsub, tail, slot_tok, h, w_gu, w_gu, w_down, b_gu, b_gu, b_down)


def _combine_ln2_kernel(pos_ref, h_ref, hb_ref, p_ref, gate_ref, wg_ref, bg_ref, wp_ref, g_ref, b_ref, y_hbm,
                        x_ref, xb_ref, yg, sem, *, tm):
    i = pl.program_id(0)
    base = i * (tm * TOP_K)

    def issue(r8, c):
        for u in range(8):
            for k in range(TOP_K):
                s = pos_ref[base + (r8 * 8 + u) * TOP_K + k]
                pltpu.make_async_copy(y_hbm.at[pl.ds(s, 1)], yg.at[k, r8, pl.ds(u, 1)], sem).start()
        return c
    lax.fori_loop(0, tm // 8, issue, 0)

    gate_v = jax.nn.sigmoid(jnp.dot(hb_ref[...], wg_ref[...], preferred_element_type=_F32) + bg_ref[...])
    pp = jnp.dot(p_ref[...].astype(_BF), wp_ref[...], preferred_element_type=_F32)
    acc = DEEPNORM_ALPHA * h_ref[...] + gate_v * pp

    def wait_row(r, c):
        pltpu.make_async_copy(y_hbm.at[pl.ds(0, 1)], yg.at[0, 0, pl.ds(0, 1)], sem).wait()
        return c
    lax.fori_loop(0, tm * TOP_K, wait_row, 0, unroll=8)

    gates = gate_ref[...]
    for k in range(TOP_K):
        acc = acc + gates[:, k:k + 1] * yg[k].reshape(acc.shape)
    xn = _ln(acc, g_ref[...], b_ref[...])
    x_ref[...] = xn
    xb_ref[...] = xn.astype(_BF)


def _combine_ln2(h, hb, p_i, gates, pos, y_sorted, w_g_b, b_g, w_p_b, ln_g, ln_b):
    n, d = h.shape
    tm = C2_TM
    row = lambda i, pos_ref: (i, 0)
    const = lambda i, pos_ref: (0, 0)
    grid_spec = pltpu.PrefetchScalarGridSpec(
        num_scalar_prefetch=1,
        grid=(n // tm,),
        in_specs=[
            pl.BlockSpec((tm, d), row), pl.BlockSpec((tm, d), row), pl.BlockSpec((tm, PLE_DIM), row),
            pl.BlockSpec((tm, TOP_K), row),
            pl.BlockSpec((d, d), const), pl.BlockSpec((1, d), const), pl.BlockSpec((PLE_DIM, d), const),
            pl.BlockSpec((1, d), const), pl.BlockSpec((1, d), const),
            pl.BlockSpec(memory_space=pl.ANY),
        ],
        out_specs=[pl.BlockSpec((tm, d), row), pl.BlockSpec((tm, d), row)],
        scratch_shapes=[pltpu.VMEM((TOP_K, tm // 8, 8, d), _F32), pltpu.SemaphoreType.DMA(())],
    )
    return pl.pallas_call(
        functools.partial(_combine_ln2_kernel, tm=tm),
        grid_spec=grid_spec,
        out_shape=[jax.ShapeDtypeStruct((n, d), _F32), jax.ShapeDtypeStruct((n, d), _BF)],
        compiler_params=_params(("arbitrary",)),
        name="combine_ln2",
    )(pos.reshape(-1), h, hb, p_i, gates, w_g_b, b_g, w_p_b, ln_g, ln_b, y_sorted)


def _mm_kernel(a_ref, w_ref, o_ref):
    o_ref[...] = jnp.dot(a_ref[...], w_ref[...], preferred_element_type=_F32)


def _mm(a_b, w_b):
    n, d = a_b.shape
    nout = w_b.shape[1]
    tm, tn = MM_TM, MM_TN
    return pl.pallas_call(
        _mm_kernel,
        grid=(n // tm, nout // tn),
        in_specs=[pl.BlockSpec((tm, d), lambda i, j: (i, 0)), pl.BlockSpec((d, tn), lambda i, j: (0, j))],
        out_specs=pl.BlockSpec((tm, tn), lambda i, j: (i, j)),
        out_shape=jax.ShapeDtypeStruct((n, nout), _F32),
        compiler_params=_params(("parallel", "arbitrary")),
        name="mm",
    )(a_b, w_b)


def _attn_kernel(slopes_ref, q0_ref, q1_ref, q2_ref, kc_ref, kp_ref, vc_ref, vp_ref, o_ref,
                 bias_s, m_s, l_s, acc_s):
    j = pl.program_id(1)
    h = pl.program_id(2)
    neg_slope = -slopes_ref[h]
    qi = lax.broadcasted_iota(jnp.int32, (QBLK, 2 * QBLK), 0)
    ki = lax.broadcasted_iota(jnp.int32, (QBLK, 2 * QBLK), 1)
    dist = qi + QBLK - ki
    n_groups = B_GROUPS
    for g, d in enumerate(B_DILATIONS):
        n_back = B_WINDOWS[g] // d
        valid = (dist >= 0) & (dist <= n_back)
        bias = jnp.where(valid, neg_slope * (dist * d).astype(_F32), NEG_INF)
        bias_s[g] = bias
        bias_s[n_groups + g] = jnp.where(j == 0, jnp.where(ki < QBLK, NEG_INF, bias), bias)

    scale = np.float32(HEAD_DIM ** -0.5)

    def block(g, d, q_ref, start, kband, vband, bias):
        q = q_ref[pl.ds(start, QBLK, stride=d), :].astype(_BF) if d > 1 else q_ref[pl.ds(start, QBLK), :].astype(_BF)
        s = lax.dot_general(q, kband, (((1,), (1,)), ((), ())), preferred_element_type=_F32)
        s = s * scale + bias
        m = jnp.max(s, axis=-1, keepdims=True)
        p = jnp.exp(s - m)
        l = jnp.sum(p, axis=-1, keepdims=True)
        pv = jnp.dot(p.astype(_BF), vband, preferred_element_type=_F32)
        mb = jnp.broadcast_to(m, (QBLK, HEAD_DIM))
        lb = jnp.broadcast_to(l, (QBLK, HEAD_DIM))
        if d > 1:
            idx = (g, pl.ds(start, QBLK, stride=d), slice(None))
        else:
            idx = (g, pl.ds(start, QBLK), slice(None))
        m_s[idx] = mb
        l_s[idx] = lb
        acc_s[idx] = pv

    def rows(ref, start, size, d):
        if d > 1:
            return ref[pl.ds(start, size, stride=d), :].astype(_BF)
        return ref[pl.ds(start, size), :].astype(_BF)

    for g, d in enumerate(B_DILATIONS):
        q_ref = (q0_ref, q1_ref, q2_ref)[g]
        span = QBLK * d
        nloc = ATT_T // span

        def first_block(r, g=g, d=d, q_ref=q_ref, span=span, nloc=nloc):
            pstart = (nloc - 1) * span + r
            kband = jnp.concatenate([rows(kp_ref, pstart, QBLK, d), rows(kc_ref, r, QBLK, d)], axis=0)
            vband = jnp.concatenate([rows(vp_ref, pstart, QBLK, d), rows(vc_ref, r, QBLK, d)], axis=0)
            block(g, d, q_ref, r, kband, vband, bias_s[n_groups + g])

        def later_block(r, n, g=g, d=d, q_ref=q_ref, span=span):
            bstart = (n - 1) * span + r
            kband = rows(kc_ref, bstart, 2 * QBLK, d)
            vband = rows(vc_ref, bstart, 2 * QBLK, d)
            block(g, d, q_ref, n * span + r, kband, vband, bias_s[g])

        if nloc == 1:
            def body(r, c, first_block=first_block):
                first_block(r)
                return c
            lax.fori_loop(0, d, body, 0, unroll=4)
        elif d == 1:
            first_block(0)

            def body(n, c, later_block=later_block):
                later_block(0, n)
                return c
            lax.fori_loop(1, nloc, body, 0, unroll=5)
        else:
            def body(r, c, first_block=first_block, later_block=later_block, nloc=nloc):
                first_block(r)
                for n in range(1, nloc):
                    later_block(r, n)
                return c
            lax.fori_loop(0, d, body, 0)

    m0, m1, m2 = m_s[0], m_s[1], m_s[2]
    mm = jnp.maximum(jnp.maximum(m0, m1), m2)
    w0 = jnp.exp(m0 - mm)
    w1 = jnp.exp(m1 - mm)
    w2 = jnp.exp(m2 - mm)
    num = w0 * acc_s[0] + w1 * acc_s[1] + w2 * acc_s[2]
    den = w0 * l_s[0] + w1 * l_s[1] + w2 * l_s[2]
    o_ref[...] = (num / den).astype(o_ref.dtype)


def _dilated_attention(q, kv, bsz, seq):
    n = q.shape[0]
    t = ATT_T
    tiles = seq // t
    hd = B_HEADS
    slopes = 2.0 ** (-8.0 * jnp.arange(1, hd + 1, dtype=_F32) / hd)

    def qspec(g):
        return pl.BlockSpec((t, HEAD_DIM), lambda b, j, h, s, g=g: (b * tiles + j, g * hd + h))

    def cur(off):
        return pl.BlockSpec((t, HEAD_DIM), lambda b, j, h, s: (b * tiles + j, off + h))

    def prev(off):
        return pl.BlockSpec((t, HEAD_DIM), lambda b, j, h, s: (b * tiles + jnp.maximum(j - 1, 0), off + h))

    grid_spec = pltpu.PrefetchScalarGridSpec(
        num_scalar_prefetch=1,
        grid=(bsz, tiles, hd),
        in_specs=[qspec(0), qspec(1), qspec(2), cur(0), prev(0), cur(hd), prev(hd)],
        out_specs=pl.BlockSpec((t, HEAD_DIM), lambda b, j, h, s: (b * tiles + j, h)),
        scratch_shapes=[
            pltpu.VMEM((2 * B_GROUPS, QBLK, 2 * QBLK), _F32),
            pltpu.VMEM((B_GROUPS, t, HEAD_DIM), _F32),
            pltpu.VMEM((B_GROUPS, t, HEAD_DIM), _F32),
            pltpu.VMEM((B_GROUPS, t, HEAD_DIM), _F32),
        ],
    )
    return pl.pallas_call(
        _attn_kernel,
        grid_spec=grid_spec,
        out_shape=jax.ShapeDtypeStruct((n, hd * HEAD_DIM), _BF),
        compiler_params=_params(("parallel", "parallel", "arbitrary")),
        name="dilated_attn",
    )(slopes, q, q, q, kv, kv, kv, kv)


def kernel(x, p, a_w_in, a_b_in, a_ln_g, a_ln_b, a_w_s, a_b_s, a_w_out, a_b_out, kv_w, b_w_q, b_w_o, ln1_g, ln1_b, ln2_g, ln2_b, router_w, router_b, moe_w_gu, moe_b_gu, moe_w_down, moe_b_down, ple_w_p, ple_w_g, ple_b_g):
    bsz, seq, d = x.shape
    n = bsz * seq
    xf = x.reshape(n, d)
    xb = xf.astype(_BF)
    row = lambda v: v.reshape(1, -1)
    kv = None
    for i in range(DEPTH):
        if i < N_A_LAYERS:
            gated = _gmlp_front(xb, a_w_in[i].astype(_BF), row(a_b_in[i]), row(a_ln_g[i]), row(a_ln_b[i]),
                                a_w_s[i], a_b_s[i].T)
            h, hb, idx, gates = _proj_ln(gated, a_w_out[i].astype(_BF), row(a_b_out[i]), xf,
                                         row(ln1_g[i]), row(ln1_b[i]), router_w[i], row(router_b[i]))
        else:
            jl = i - N_A_LAYERS
            q = _mm(xb, b_w_q[jl].astype(_BF))
            att = _dilated_attention(q, kv, bsz, seq)
            h, hb, idx, gates = _proj_ln(att, b_w_o[jl].astype(_BF), None, xf,
                                         row(ln1_g[i]), row(ln1_b[i]), router_w[i], row(router_b[i]))
        pos, slot_tok, ge, gs, nsub, tail = _routing(idx)
        y_sorted = _moe_experts(h, moe_w_gu, moe_b_gu, moe_w_down, moe_b_down, i,
                                slot_tok, ge, gs, nsub, tail)
        xf, xb = _combine_ln2(h, hb, p[i].reshape(n, PLE_DIM), gates, pos, y_sorted,
                              ple_w_g[i].astype(_BF), row(ple_b_g[i]), ple_w_p[i].astype(_BF),
                              row(ln2_g[i]), row(ln2_b[i]))
        if i == N_A_LAYERS - 1:
            kv = _mm(xb, kv_w.astype(_BF))
    return xf.reshape(bsz, seq, d)
```

```python
import functools

import jax
import jax.numpy as jnp
import numpy as np
from jax import lax
from jax.experimental import pallas as pl
from jax.experimental.pallas import tpu as pltpu

D_MODEL = 2048
DEPTH = 4
N_A_LAYERS = DEPTH // 2
CHUNK = 128
A_WIDTH = D_MODEL
A_GROUPS = 8
A_GROUP_CH = A_WIDTH // A_GROUPS
HEAD_DIM = 128
B_HEADS = D_MODEL // HEAD_DIM
B_WINDOWS = (128, 512, 2048)
B_DILATIONS = (1, 4, 16)
B_GROUPS = len(B_WINDOWS)
QBLK = 128
N_EXPERTS = 32
TOP_K = 4
EXPERT_FF = D_MODEL // 2
SWIGLU_LIMIT = 7.0
SWIGLU_ALPHA = 1.702
PLE_DIM = 256
LN_EPS = 1e-5
DEEPNORM_ALPHA = (2.0 * DEPTH) ** 0.25
NEG_INF = -1e30

V7X_VMEM_BYTES = 64 * 1024 * 1024
VMEM_LIMIT = 56 * 1024 * 1024
GF_TM, GF_TK = 512, 512
PL_TM = 256
MM_TM, MM_TN = 1024, 512
ATT_T = QBLK * max(B_DILATIONS)
MOE_F = 256
MOE_J = EXPERT_FF // MOE_F
MOE_SUB = 256
MOE_RMAX = 1280
C2_TM = 256
C2_CHUNKS = 4

_BF = jnp.bfloat16
_F32 = jnp.float32


def _params(sem):
    return pltpu.CompilerParams(dimension_semantics=sem, vmem_limit_bytes=VMEM_LIMIT)


def _ln(y, g, b):
    mu = jnp.mean(y, axis=-1, keepdims=True)
    yc = y - mu
    var = jnp.mean(yc * yc, axis=-1, keepdims=True)
    return yc * lax.rsqrt(var + LN_EPS) * g + b


def _gmlp_front_kernel(x_ref, w_ref, b_ref, lng_ref, lnb_ref, ws_ref, bst_ref, o_ref, acc_ref, *, nk, tm):
    k = pl.program_id(1)

    @pl.when(k == 0)
    def _():
        acc_ref[...] = jnp.zeros_like(acc_ref)

    acc_ref[...] += jnp.dot(x_ref[...], w_ref[...], preferred_element_type=_F32)

    @pl.when(k == nk - 1)
    def _():
        row = lax.broadcasted_iota(jnp.int32, (CHUNK, CHUNK), 0)
        col = lax.broadcasted_iota(jnp.int32, (CHUNK, CHUNK), 1)
        causal = row >= col
        sqrt_half = np.float32(np.sqrt(0.5))

        def chunk(c, carry):
            r0 = pl.multiple_of(c * CHUNK, CHUNK)
            z = acc_ref[pl.ds(r0, CHUNK), :] + b_ref[...]
            z = 0.5 * z * (1.0 + lax.erf(z * sqrt_half))
            u = z[:, :A_WIDTH]
            v = _ln(z[:, A_WIDTH:], lng_ref[...], lnb_ref[...]).astype(_BF)
            for g in range(A_GROUPS):
                wg = jnp.where(causal, ws_ref[g], 0.0).astype(_BF)
                cs = slice(g * A_GROUP_CH, (g + 1) * A_GROUP_CH)
                vs = jnp.dot(wg, v[:, cs], preferred_element_type=_F32) + bst_ref[:, g:g + 1]
                o_ref[pl.ds(r0, CHUNK), cs] = (u[:, cs] * vs).astype(_BF)
            return carry

        lax.fori_loop(0, tm // CHUNK, chunk, 0)


def _gmlp_front(xb, w_in_b, b_in, ln_g, ln_b, w_s, b_s_t):
    n, d = xb.shape
    tm, tk = GF_TM, GF_TK
    nk = d // tk
    return pl.pallas_call(
        functools.partial(_gmlp_front_kernel, nk=nk, tm=tm),
        grid=(n // tm, nk),
        in_specs=[
            pl.BlockSpec((tm, tk), lambda i, k: (i, k)),
            pl.BlockSpec((tk, 2 * A_WIDTH), lambda i, k: (k, 0)),
            pl.BlockSpec((1, 2 * A_WIDTH), lambda i, k: (0, 0)),
            pl.BlockSpec((1, A_WIDTH), lambda i, k: (0, 0)),
            pl.BlockSpec((1, A_WIDTH), lambda i, k: (0, 0)),
            pl.BlockSpec((A_GROUPS, CHUNK, CHUNK), lambda i, k: (0, 0, 0)),
            pl.BlockSpec((CHUNK, A_GROUPS), lambda i, k: (0, 0)),
        ],
        out_specs=pl.BlockSpec((tm, A_WIDTH), lambda i, k: (i, 0)),
        out_shape=jax.ShapeDtypeStruct((n, A_WIDTH), _BF),
        scratch_shapes=[pltpu.VMEM((tm, 2 * A_WIDTH), _F32)],
        compiler_params=_params(("parallel", "arbitrary")),
        name="gmlp_front",
    )(xb, w_in_b, b_in, ln_g, ln_b, w_s, b_s_t)


def _proj_ln_kernel(*refs, has_bias):
    if has_bias:
        a_ref, w_ref, b_ref, xres_ref, g_ref, bb_ref, wr_ref, br_ref, h_ref, hb_ref, idx_ref, gate_ref = refs
    else:
        a_ref, w_ref, xres_ref, g_ref, bb_ref, wr_ref, br_ref, h_ref, hb_ref, idx_ref, gate_ref = refs
        b_ref = None
    mix = jnp.dot(a_ref[...], w_ref[...], preferred_element_type=_F32)
    if has_bias:
        mix = mix + b_ref[...]
    h = _ln(DEEPNORM_ALPHA * xres_ref[...] + mix, g_ref[...], bb_ref[...])
    h_ref[...] = h
    h_hi = h.astype(_BF)
    hb_ref[...] = h_hi

    h_lo = (h - h_hi.astype(_F32)).astype(_BF)
    wr = wr_ref[...]
    w_hi = wr.astype(_BF)
    w_lo = (wr - w_hi.astype(_F32)).astype(_BF)
    logits = (jnp.dot(h_hi, w_hi, preferred_element_type=_F32)
              + (jnp.dot(h_hi, w_lo, preferred_element_type=_F32)
                 + jnp.dot(h_lo, w_hi, preferred_element_type=_F32))) + br_ref[...]
    lane = lax.broadcasted_iota(jnp.int32, logits.shape, 1)
    cur = logits
    vals, idxs = [], []
    for _ in range(TOP_K):
        m = jnp.max(cur, axis=-1, keepdims=True)
        i = jnp.min(jnp.where(cur == m, lane, N_EXPERTS), axis=-1, keepdims=True)
        vals.append(m)
        idxs.append(i)
        cur = jnp.where(lane == i, -jnp.inf, cur)
    es = [jnp.exp(v - vals[0]) for v in vals]
    den = es[0] + es[1] + es[2] + es[3]
    colk = lax.broadcasted_iota(jnp.int32, idx_ref.shape, 1)
    idx_out = jnp.zeros(idx_ref.shape, jnp.int32)
    gate_out = jnp.zeros(gate_ref.shape, _F32)
    for k in range(TOP_K):
        idx_out = jnp.where(colk == k, idxs[k], idx_out)
        gate_out = jnp.where(colk == k, es[k] / den, gate_out)
    idx_ref[...] = idx_out
    gate_ref[...] = gate_out


def _proj_ln(a_b, w_b, bias, xres, ln_g, ln_b, w_r, b_r):
    n, d = xres.shape
    tm = PL_TM
    has_bias = bias is not None
    row = lambda i: (i, 0)
    const = lambda i: (0, 0)
    in_specs = [pl.BlockSpec((tm, a_b.shape[1]), row), pl.BlockSpec(w_b.shape, const)]
    args = [a_b, w_b]
    if has_bias:
        in_specs.append(pl.BlockSpec((1, d), const))
        args.append(bias)
    in_specs += [pl.BlockSpec((tm, d), row), pl.BlockSpec((1, d), const), pl.BlockSpec((1, d), const),
                 pl.BlockSpec((d, N_EXPERTS), const), pl.BlockSpec((1, N_EXPERTS), const)]
    args += [xres, ln_g, ln_b, w_r, b_r]
    return pl.pallas_call(
        functools.partial(_proj_ln_kernel, has_bias=has_bias),
        grid=(n // tm,),
        in_specs=in_specs,
        out_specs=[pl.BlockSpec((tm, d), row), pl.BlockSpec((tm, d), row),
                   pl.BlockSpec((tm, TOP_K), row), pl.BlockSpec((tm, TOP_K), row)],
        out_shape=[jax.ShapeDtypeStruct((n, d), _F32), jax.ShapeDtypeStruct((n, d), _BF),
                   jax.ShapeDtypeStruct((n, TOP_K), jnp.int32), jax.ShapeDtypeStruct((n, TOP_K), _F32)],
        compiler_params=_params(("parallel",)),
        name="proj_ln",
    )(*args)


def _moe_layout(n_tok):
    n_slots = n_tok * TOP_K + N_EXPERTS * MOE_SUB
    n_groups = N_EXPERTS + (n_tok * TOP_K) // MOE_RMAX
    return n_slots, n_groups


def _routing(idx):
    n = idx.shape[0]
    n_slots, n_groups = _moe_layout(n)
    e_ids = jnp.arange(N_EXPERTS, dtype=jnp.int32)
    onehot = (idx[:, :, None] == e_ids[None, None, :]).astype(jnp.int32).sum(axis=1)
    incl = jnp.cumsum(onehot, axis=0)
    counts = incl[-1]
    rank = jnp.take_along_axis(incl - onehot, idx, axis=1)
    padded = (counts + MOE_SUB - 1) // MOE_SUB * MOE_SUB
    pstart = jnp.cumsum(padded) - padded
    pos = (pstart[idx] + rank).astype(jnp.int32)
    tok = jnp.broadcast_to(jnp.arange(n, dtype=jnp.int32)[:, None], (n, TOP_K))
    slot_tok = jnp.zeros((n_slots,), jnp.int32).at[pos.reshape(-1)].set(tok.reshape(-1))
    ng = (counts + MOE_RMAX - 1) // MOE_RMAX
    gend = jnp.cumsum(ng)
    gstart = gend - ng
    total = gend[-1]
    gi = jnp.arange(n_groups, dtype=jnp.int32)
    gi_c = jnp.minimum(gi, total - 1)
    ge = jnp.searchsorted(gend, gi_c, side='right').astype(jnp.int32)
    local = gi_c - gstart[ge]
    gs = (pstart[ge] + local * MOE_RMAX).astype(jnp.int32)
    rows = jnp.clip(counts[ge] - local * MOE_RMAX, 0, MOE_RMAX)
    nsub = jnp.where(gi < total, (rows + MOE_SUB - 1) // MOE_SUB, 0).astype(jnp.int32)
    tail = jnp.sum(padded).astype(jnp.int32).reshape(1)
    return pos, slot_tok, ge, gs, nsub, tail


def _moe_kernel(ge_ref, gs_ref, nsub_ref, tail_ref, tok_ref,
                h_hbm, wg_ref, wu_ref, wd_ref, bg_ref, bu_ref, bd_ref,
                y_hbm,
                xg, xb, yacc, wgb, wub, wdb, issued, sem_g, sem_o):
    g = pl.program_id(0)
    j = pl.program_id(1)
    n_grp = pl.num_programs(0)
    ns = nsub_ref[g]
    g_next = jnp.minimum(g + 1, n_grp - 1)
    next_rows = jnp.where(g + 1 < n_grp, nsub_ref[g_next], 0) * MOE_SUB
    next_base = gs_ref[g_next]

    def gather_copy(t, i8, u):
        return pltpu.make_async_copy(h_hbm.at[pl.ds(t, 1)], xg.at[i8, pl.ds(u, 1)], sem_g)

    def issue8(slot_base, t8):
        for u in range(8):
            gather_copy(tok_ref[slot_base + t8 * 8 + u], t8, u).start()

    def out_copy(base, r0):
        dst = pl.multiple_of(base + r0, MOE_SUB)
        return pltpu.make_async_copy(yacc.at[pl.ds(r0, MOE_SUB)], y_hbm.at[pl.ds(dst, MOE_SUB)], sem_o)

    @pl.when((g == 0) & (j == 0))
    def _():
        issued[0] = 0

    @pl.when(ns > 0)
    def _():
        base = pl.multiple_of(gs_ref[g], MOE_SUB)

        @pl.when(j == 0)
        def _():
            def issue(t8, c):
                issue8(base, t8)
                return c
            lax.fori_loop(issued[0] // 8, ns * (MOE_SUB // 8), issue, 0)
            issued[0] = 0

            def wait_row(i, c):
                gather_copy(0, 0, 0).wait()
                return c

            def wait_sub(s, c):
                return lax.fori_loop(0, MOE_SUB, wait_row, c, unroll=8)
            lax.fori_loop(0, ns, wait_sub, 0)

            bias_rows = jnp.broadcast_to(bd_ref[0], (MOE_SUB, yacc.shape[1]))

            def cast(s, c):
                r0 = pl.multiple_of(s * MOE_SUB, MOE_SUB)
                t0 = s * (MOE_SUB // 8)
                xs = xg[pl.ds(t0, MOE_SUB // 8)].reshape(MOE_SUB, xb.shape[1])
                xb[pl.ds(r0, MOE_SUB), :] = xs.astype(_BF)
                yacc[pl.ds(r0, MOE_SUB), :] = bias_rows
                return c
            lax.fori_loop(0, ns, cast, 0)

        wgb[...] = wg_ref[0].astype(_BF)
        wub[...] = wu_ref[0].astype(_BF)
        wdb[...] = wd_ref[0].astype(_BF)

        def compute(r0):
            xs = xb[pl.ds(r0, MOE_SUB), :]
            gg = jnp.dot(xs, wgb[...], preferred_element_type=_F32) + bg_ref[0]
            uu = jnp.dot(xs, wub[...], preferred_element_type=_F32) + bu_ref[0]
            gg = jnp.minimum(gg, SWIGLU_LIMIT)
            uu = jnp.clip(uu, -SWIGLU_LIMIT, SWIGLU_LIMIT)
            act = (uu + 1.0) * (gg * jax.nn.sigmoid(SWIGLU_ALPHA * gg))
            yacc[pl.ds(r0, MOE_SUB), :] += jnp.dot(act.astype(_BF), wdb[...], preferred_element_type=_F32)

        def step(s0, n_sub):
            def body(prefetch):
                if prefetch:
                    t8_0 = issued[0] // 8
                    for i8 in range(MOE_SUB // 8):
                        issue8(next_base, t8_0 + i8)
                    issued[0] = issued[0] + MOE_SUB
                for k in range(n_sub):
                    compute(pl.multiple_of((s0 + k) * MOE_SUB, MOE_SUB))

            lax.cond(issued[0] < next_rows, lambda: body(True), lambda: body(False))

            @pl.when(j == MOE_J - 1)
            def _():
                for k in range(n_sub):
                    out_copy(base, pl.multiple_of((s0 + k) * MOE_SUB, MOE_SUB)).start()

        def pair(p, c):
            step(2 * p, 2)
            return c
        lax.fori_loop(0, ns // 2, pair, 0)

        @pl.when(ns % 2 == 1)
        def _():
            step(ns - 1, 1)

        @pl.when(j == MOE_J - 1)
        def _():
            def drain(s, c):
                out_copy(base, pl.multiple_of(s * MOE_SUB, MOE_SUB)).wait()
                return c
            lax.fori_loop(0, ns, drain, 0)

    @pl.when((g == pl.num_programs(0) - 1) & (j == MOE_J - 1))
    def _():
        tail = pl.multiple_of(tail_ref[0], MOE_SUB)
        n_tail = (y_hbm.shape[0] - tail) // MOE_SUB
        yacc[pl.ds(0, MOE_SUB), :] = jnp.zeros((MOE_SUB, yacc.shape[1]), _F32)

        def tail_copy(s):
            dst = pl.multiple_of(tail + s * MOE_SUB, MOE_SUB)
            return pltpu.make_async_copy(yacc.at[pl.ds(0, MOE_SUB)], y_hbm.at[pl.ds(dst, MOE_SUB)], sem_o)

        def start(s, c):
            tail_copy(s).start()
            return c
        lax.fori_loop(0, n_tail, start, 0)

        def done(s, c):
            tail_copy(s).wait()
            return c
        lax.fori_loop(0, n_tail, done, 0)


def _moe_experts(h, w_gu, b_gu, w_down, b_down, layer, slot_tok, ge, gs, nsub, tail):
    n, d = h.shape
    n_rows = w_gu.shape[0] * N_EXPERTS
    w_gu = w_gu.reshape(n_rows, d, 2 * EXPERT_FF)
    w_down = w_down.reshape(n_rows, EXPERT_FF, d)
    b_gu = b_gu.reshape(n_rows, 1, 2 * EXPERT_FF)
    b_down = b_down.reshape(n_rows, 1, d)
    ge = ge + layer * N_EXPERTS
    n_slots, n_groups = _moe_layout(n)
    f, jn = MOE_F, MOE_J

    def jj(g, j, nsub_ref):
        return jnp.where(nsub_ref[g] > 0, j, jn - 1)

    grid_spec = pltpu.PrefetchScalarGridSpec(
        num_scalar_prefetch=5,
        grid=(n_groups, jn),
        in_specs=[
            pl.BlockSpec(memory_space=pl.ANY),
            pl.BlockSpec((1, d, f), lambda g, j, ge, gs, ns, tl, tk: (ge[g], 0, jj(g, j, ns))),
            pl.BlockSpec((1, d, f), lambda g, j, ge, gs, ns, tl, tk: (ge[g], 0, jn + jj(g, j, ns))),
            pl.BlockSpec((1, f, d), lambda g, j, ge, gs, ns, tl, tk: (ge[g], jj(g, j, ns), 0)),
            pl.BlockSpec((1, 1, f), lambda g, j, ge, gs, ns, tl, tk: (ge[g], 0, jj(g, j, ns))),
            pl.BlockSpec((1, 1, f), lambda g, j, ge, gs, ns, tl, tk: (ge[g], 0, jn + jj(g, j, ns))),
            pl.BlockSpec((1, 1, d), lambda g, j, ge, gs, ns, tl, tk: (ge[g], 0, 0)),
        ],
        out_specs=pl.BlockSpec(memory_space=pl.ANY),
        scratch_shapes=[
            pltpu.VMEM((MOE_RMAX // 8, 8, d), _F32),
            pltpu.VMEM((MOE_RMAX, d), _BF),
            pltpu.VMEM((MOE_RMAX, d), _F32),
            pltpu.VMEM((d, f), _BF),
            pltpu.VMEM((d, f), _BF),
            pltpu.VMEM((f, d), _BF),
            pltpu.SMEM((1,), jnp.int32),
            pltpu.SemaphoreType.DMA(()),
            pltpu.SemaphoreType.DMA(()),
        ],
    )
    return pl.pallas_call(
        _moe_kernel,
        grid_spec=grid_spec,
        out_shape=jax.ShapeDtypeStruct((n_slots, d), _F32),
        compiler_params=_params(("arbitrary", "arbitrary")),
        name="moe_experts",
    )(ge, gs, nsub, tail, slot_tok, h, w_gu, w_gu, w_down, b_gu, b_gu, b_down)


def _combine_ln2_kernel(pos_ref, h_ref, hb_ref, p_ref, gate_ref, wg_ref, bg_ref, wp_ref, g_ref, b_ref, y_hbm,
                        x_ref, xb_ref, yg, sem, *, tm):
    i = pl.program_id(0)
    base = i * (tm * TOP_K)

    def issue(r8):
        for u in range(8):
            for k in range(TOP_K):
                s = pos_ref[base + (r8 * 8 + u) * TOP_K + k]
                pltpu.make_async_copy(y_hbm.at[pl.ds(s, 1)], yg.at[k, r8, pl.ds(u, 1)], sem).start()

    d = h_ref.shape[1]
    n_ch = C2_CHUNKS
    cw = d // n_ch
    hb = hb_ref[...]
    pb = p_ref[...].astype(_BF)
    parts = []
    for c in range(n_ch):
        for r8 in range(c * (tm // 8 // n_ch), (c + 1) * (tm // 8 // n_ch)):
            issue(r8)
        cs = slice(c * cw, (c + 1) * cw)
        gate_v = jax.nn.sigmoid(jnp.dot(hb, wg_ref[:, cs], preferred_element_type=_F32) + bg_ref[:, cs])
        pp = jnp.dot(pb, wp_ref[:, cs], preferred_element_type=_F32)
        parts.append(DEEPNORM_ALPHA * h_ref[:, cs] + gate_v * pp)
    acc = jnp.concatenate(parts, axis=1)

    def wait_row(r, c):
        pltpu.make_async_copy(y_hbm.at[pl.ds(0, 1)], yg.at[0, 0, pl.ds(0, 1)], sem).wait()
        return c
    lax.fori_loop(0, tm * TOP_K, wait_row, 0, unroll=8)

    gates = gate_ref[...]
    for k in range(TOP_K):
        acc = acc + gates[:, k:k + 1] * yg[k].reshape(acc.shape)
    xn = _ln(acc, g_ref[...], b_ref[...])
    x_ref[...] = xn
    xb_ref[...] = xn.astype(_BF)


def _combine_ln2(h, hb, p_i, gates, pos, y_sorted, w_g_b, b_g, w_p_b, ln_g, ln_b):
    n, d = h.shape
    tm = C2_TM
    row = lambda i, pos_ref: (i, 0)
    const = lambda i, pos_ref: (0, 0)
    grid_spec = pltpu.PrefetchScalarGridSpec(
        num_scalar_prefetch=1,
        grid=(n // tm,),
        in_specs=[
            pl.BlockSpec((tm, d), row), pl.BlockSpec((tm, d), row), pl.BlockSpec((tm, PLE_DIM), row),
            pl.BlockSpec((tm, TOP_K), row),
            pl.BlockSpec((d, d), const), pl.BlockSpec((1, d), const), pl.BlockSpec((PLE_DIM, d), const),
            pl.BlockSpec((1, d), const), pl.BlockSpec((1, d), const),
            pl.BlockSpec(memory_space=pl.ANY),
        ],
        out_specs=[pl.BlockSpec((tm, d), row), pl.BlockSpec((tm, d), row)],
        scratch_shapes=[pltpu.VMEM((TOP_K, tm // 8, 8, d), _F32), pltpu.SemaphoreType.DMA(())],
    )
    return pl.pallas_call(
        functools.partial(_combine_ln2_kernel, tm=tm),
        grid_spec=grid_spec,
        out_shape=[jax.ShapeDtypeStruct((n, d), _F32), jax.ShapeDtypeStruct((n, d), _BF)],
        compiler_params=_params(("arbitrary",)),
        name="combine_ln2",
    )(pos.reshape(-1), h, hb, p_i, gates, w_g_b, b_g, w_p_b, ln_g, ln_b, y_sorted)


def _mm_kernel(a_ref, w_ref, o_ref):
    o_ref[...] = jnp.dot(a_ref[...], w_ref[...], preferred_element_type=_F32)


def _mm(a_b, w_b):
    n, d = a_b.shape
    nout = w_b.shape[1]
    tm, tn = MM_TM, MM_TN
    return pl.pallas_call(
        _mm_kernel,
        grid=(n // tm, nout // tn),
        in_specs=[pl.BlockSpec((tm, d), lambda i, j: (i, 0)), pl.BlockSpec((d, tn), lambda i, j: (0, j))],
        out_specs=pl.BlockSpec((tm, tn), lambda i, j: (i, j)),
        out_shape=jax.ShapeDtypeStruct((n, nout), _F32),
        compiler_params=_params(("parallel", "arbitrary")),
        name="mm",
    )(a_b, w_b)


def _attn_kernel(slopes_ref, q0_ref, q1_ref, q2_ref, kc_ref, kp_ref, vc_ref, vp_ref, o_ref,
                 bias_s, m_s, l_s, acc_s):
    j = pl.program_id(1)
    h = pl.program_id(2)
    neg_slope = -slopes_ref[h]
    qi = lax.broadcasted_iota(jnp.int32, (QBLK, 2 * QBLK), 0)
    ki = lax.broadcasted_iota(jnp.int32, (QBLK, 2 * QBLK), 1)
    dist = qi + QBLK - ki
    n_groups = B_GROUPS
    for g, d in enumerate(B_DILATIONS):
        n_back = B_WINDOWS[g] // d
        valid = (dist >= 0) & (dist <= n_back)
        bias = jnp.where(valid, neg_slope * (dist * d).astype(_F32), NEG_INF)
        bias_s[g] = bias
        bias_s[n_groups + g] = jnp.where(j == 0, jnp.where(ki < QBLK, NEG_INF, bias), bias)

    scale = np.float32(HEAD_DIM ** -0.5)

    def block(g, d, q_ref, start, kband, vband, bias):
        q = q_ref[pl.ds(start, QBLK, stride=d), :].astype(_BF) if d > 1 else q_ref[pl.ds(start, QBLK), :].astype(_BF)
        s = lax.dot_general(q, kband, (((1,), (1,)), ((), ())), preferred_element_type=_F32)
        s = s * scale + bias
        m = jnp.max(s, axis=-1, keepdims=True)
        p = jnp.exp(s - m)
        l = jnp.sum(p, axis=-1, keepdims=True)
        pv = jnp.dot(p.astype(_BF), vband, preferred_element_type=_F32)
        mb = jnp.broadcast_to(m, (QBLK, HEAD_DIM))
        lb = jnp.broadcast_to(l, (QBLK, HEAD_DIM))
        if d > 1:
            idx = (g, pl.ds(start, QBLK, stride=d), slice(None))
        else:
            idx = (g, pl.ds(start, QBLK), slice(None))
        m_s[idx] = mb
        l_s[idx] = lb
        acc_s[idx] = pv

    def rows(ref, start, size, d):
        if d > 1:
            return ref[pl.ds(start, size, stride=d), :].astype(_BF)
        return ref[pl.ds(start, size), :].astype(_BF)

    for g, d in enumerate(B_DILATIONS):
        q_ref = (q0_ref, q1_ref, q2_ref)[g]
        span = QBLK * d
        nloc = ATT_T // span

        def first_block(r, g=g, d=d, q_ref=q_ref, span=span, nloc=nloc):
            pstart = (nloc - 1) * span + r
            kband = jnp.concatenate([rows(kp_ref, pstart, QBLK, d), rows(kc_ref, r, QBLK, d)], axis=0)
            vband = jnp.concatenate([rows(vp_ref, pstart, QBLK, d), rows(vc_ref, r, QBLK, d)], axis=0)
            block(g, d, q_ref, r, kband, vband, bias_s[n_groups + g])

        def later_block(r, n, g=g, d=d, q_ref=q_ref, span=span):
            bstart = (n - 1) * span + r
            kband = rows(kc_ref, bstart, 2 * QBLK, d)
            vband = rows(vc_ref, bstart, 2 * QBLK, d)
            block(g, d, q_ref, n * span + r, kband, vband, bias_s[g])

        if nloc == 1:
            def body(r, c, first_block=first_block):
                first_block(r)
                return c
            lax.fori_loop(0, d, body, 0, unroll=4)
        elif d == 1:
            first_block(0)

            def body(n, c, later_block=later_block):
                later_block(0, n)
                return c
            lax.fori_loop(1, nloc, body, 0, unroll=5)
        else:
            def body(r, c, first_block=first_block, later_block=later_block, nloc=nloc):
                first_block(r)
                for n in range(1, nloc):
                    later_block(r, n)
                return c
            lax.fori_loop(0, d, body, 0)

    m0, m1, m2 = m_s[0], m_s[1], m_s[2]
    mm = jnp.maximum(jnp.maximum(m0, m1), m2)
    w0 = jnp.exp(m0 - mm)
    w1 = jnp.exp(m1 - mm)
    w2 = jnp.exp(m2 - mm)
    num = w0 * acc_s[0] + w1 * acc_s[1] + w2 * acc_s[2]
    den = w0 * l_s[0] + w1 * l_s[1] + w2 * l_s[2]
    o_ref[...] = (num / den).astype(o_ref.dtype)


def _dilated_attention(q, kv, bsz, seq):
    n = q.shape[0]
    t = ATT_T
    tiles = seq // t
    hd = B_HEADS
    slopes = 2.0 ** (-8.0 * jnp.arange(1, hd + 1, dtype=_F32) / hd)

    def qspec(g):
        return pl.BlockSpec((t, HEAD_DIM), lambda b, j, h, s, g=g: (b * tiles + j, g * hd + h))

    def cur(off):
        return pl.BlockSpec((t, HEAD_DIM), lambda b, j, h, s: (b * tiles + j, off + h))

    def prev(off):
        return pl.BlockSpec((t, HEAD_DIM), lambda b, j, h, s: (b * tiles + jnp.maximum(j - 1, 0), off + h))

    grid_spec = pltpu.PrefetchScalarGridSpec(
        num_scalar_prefetch=1,
        grid=(bsz, tiles, hd),
        in_specs=[qspec(0), qspec(1), qspec(2), cur(0), prev(0), cur(hd), prev(hd)],
        out_specs=pl.BlockSpec((t, HEAD_DIM), lambda b, j, h, s: (b * tiles + j, h)),
        scratch_shapes=[
            pltpu.VMEM((2 * B_GROUPS, QBLK, 2 * QBLK), _F32),
            pltpu.VMEM((B_GROUPS, t, HEAD_DIM), _F32),
            pltpu.VMEM((B_GROUPS, t, HEAD_DIM), _F32),
            pltpu.VMEM((B_GROUPS, t, HEAD_DIM), _F32),
        ],
    )
    return pl.pallas_call(
        _attn_kernel,
        grid_spec=grid_spec,
        out_shape=jax.ShapeDtypeStruct((n, hd * HEAD_DIM), _BF),
        compiler_params=_params(("parallel", "parallel", "arbitrary")),
        name="dilated_attn",
    )(slopes, q, q, q, kv, kv, kv, kv)


def kernel(x, p, a_w_in, a_b_in, a_ln_g, a_ln_b, a_w_s, a_b_s, a_w_out, a_b_out, kv_w, b_w_q, b_w_o, ln1_g, ln1_b, ln2_g, ln2_b, router_w, router_b, moe_w_gu, moe_b_gu, moe_w_down, moe_b_down, ple_w_p, ple_w_g, ple_b_g):
    bsz, seq, d = x.shape
    n = bsz * seq
    xf = x.reshape(n, d)
    xb = xf.astype(_BF)
    row = lambda v: v.reshape(1, -1)
    kv = None
    for i in range(DEPTH):
        if i < N_A_LAYERS:
            gated = _gmlp_front(xb, a_w_in[i].astype(_BF), row(a_b_in[i]), row(a_ln_g[i]), row(a_ln_b[i]),
                                a_w_s[i], a_b_s[i].T)
            h, hb, idx, gates = _proj_ln(gated, a_w_out[i].astype(_BF), row(a_b_out[i]), xf,
                                         row(ln1_g[i]), row(ln1_b[i]), router_w[i], row(router_b[i]))
        else:
            jl = i - N_A_LAYERS
            q = _mm(xb, b_w_q[jl].astype(_BF))
            att = _dilated_attention(q, kv, bsz, seq)
            h, hb, idx, gates = _proj_ln(att, b_w_o[jl].astype(_BF), None, xf,
                                         row(ln1_g[i]), row(ln1_b[i]), router_w[i], row(router_b[i]))
        pos, slot_tok, ge, gs, nsub, tail = _routing(idx)
        y_sorted = _moe_experts(h, moe_w_gu, moe_b_gu, moe_w_down, moe_b_down, i,
                                slot_tok, ge, gs, nsub, tail)
        xf, xb = _combine_ln2(h, hb, p[i].reshape(n, PLE_DIM), gates, pos, y_sorted,
                              ple_w_g[i].astype(_BF), row(ple_b_g[i]), ple_w_p[i].astype(_BF),
                              row(ln2_g[i]), row(ln2_b[i]))
        if i == N_A_LAYERS - 1:
            kv = _mm(xb, kv_w.astype(_BF))
    return xf.reshape(bsz, seq, d)
```

```python
import functools

import jax
import jax.numpy as jnp
import numpy as np
from jax import lax
from jax.experimental import pallas as pl
from jax.experimental.pallas import tpu as pltpu

D_MODEL = 2048
DEPTH = 4
N_A_LAYERS = DEPTH // 2
CHUNK = 128
A_WIDTH = D_MODEL
A_GROUPS = 8
A_GROUP_CH = A_WIDTH // A_GROUPS
HEAD_DIM = 128
B_HEADS = D_MODEL // HEAD_DIM
B_WINDOWS = (128, 512, 2048)
B_DILATIONS = (1, 4, 16)
B_GROUPS = len(B_WINDOWS)
QBLK = 128
N_EXPERTS = 32
TOP_K = 4
EXPERT_FF = D_MODEL // 2
SWIGLU_LIMIT = 7.0
SWIGLU_ALPHA = 1.702
PLE_DIM = 256
LN_EPS = 1e-5
DEEPNORM_ALPHA = (2.0 * DEPTH) ** 0.25
NEG_INF = -1e30

V7X_VMEM_BYTES = 64 * 1024 * 1024
VMEM_LIMIT = 56 * 1024 * 1024
GF_TM, GF_TK = 512, 512
PL_TM = 512
PL_PARTS = 2
MM_TM, MM_TN = 1024, 512
ATT_T = QBLK * max(B_DILATIONS)
MOE_F = 256
MOE_J = EXPERT_FF // MOE_F
MOE_SUB = 256
MOE_RMAX = 1280
C2_TM = 256
C2_CHUNKS = 4

_BF = jnp.bfloat16
_F32 = jnp.float32


def _params(sem):
    return pltpu.CompilerParams(dimension_semantics=sem, vmem_limit_bytes=VMEM_LIMIT)


def _ln(y, g, b):
    mu = jnp.mean(y, axis=-1, keepdims=True)
    yc = y - mu
    var = jnp.mean(yc * yc, axis=-1, keepdims=True)
    return yc * lax.rsqrt(var + LN_EPS) * g + b


def _gmlp_front_kernel(x_ref, w_ref, b_ref, lng_ref, lnb_ref, ws_ref, bst_ref, o_ref, acc_ref, *, nk, tm):
    k = pl.program_id(1)

    @pl.when(k == 0)
    def _():
        acc_ref[...] = jnp.zeros_like(acc_ref)

    acc_ref[...] += jnp.dot(x_ref[...], w_ref[...], preferred_element_type=_F32)

    @pl.when(k == nk - 1)
    def _():
        row = lax.broadcasted_iota(jnp.int32, (CHUNK, CHUNK), 0)
        col = lax.broadcasted_iota(jnp.int32, (CHUNK, CHUNK), 1)
        causal = row >= col
        sqrt_half = np.float32(np.sqrt(0.5))

        def chunk(c, carry):
            r0 = pl.multiple_of(c * CHUNK, CHUNK)
            z = acc_ref[pl.ds(r0, CHUNK), :] + b_ref[...]
            z = 0.5 * z * (1.0 + lax.erf(z * sqrt_half))
            u = z[:, :A_WIDTH]
            v = _ln(z[:, A_WIDTH:], lng_ref[...], lnb_ref[...]).astype(_BF)
            for g in range(A_GROUPS):
                wg = jnp.where(causal, ws_ref[g], 0.0).astype(_BF)
                cs = slice(g * A_GROUP_CH, (g + 1) * A_GROUP_CH)
                vs = jnp.dot(wg, v[:, cs], preferred_element_type=_F32) + bst_ref[:, g:g + 1]
                o_ref[pl.ds(r0, CHUNK), cs] = (u[:, cs] * vs).astype(_BF)
            return carry

        lax.fori_loop(0, tm // CHUNK, chunk, 0)


def _gmlp_front(xb, w_in_b, b_in, ln_g, ln_b, w_s, b_s_t):
    n, d = xb.shape
    tm, tk = GF_TM, GF_TK
    nk = d // tk
    return pl.pallas_call(
        functools.partial(_gmlp_front_kernel, nk=nk, tm=tm),
        grid=(n // tm, nk),
        in_specs=[
            pl.BlockSpec((tm, tk), lambda i, k: (i, k)),
            pl.BlockSpec((tk, 2 * A_WIDTH), lambda i, k: (k, 0)),
            pl.BlockSpec((1, 2 * A_WIDTH), lambda i, k: (0, 0)),
            pl.BlockSpec((1, A_WIDTH), lambda i, k: (0, 0)),
            pl.BlockSpec((1, A_WIDTH), lambda i, k: (0, 0)),
            pl.BlockSpec((A_GROUPS, CHUNK, CHUNK), lambda i, k: (0, 0, 0)),
            pl.BlockSpec((CHUNK, A_GROUPS), lambda i, k: (0, 0)),
        ],
        out_specs=pl.BlockSpec((tm, A_WIDTH), lambda i, k: (i, 0)),
        out_shape=jax.ShapeDtypeStruct((n, A_WIDTH), _BF),
        scratch_shapes=[pltpu.VMEM((tm, 2 * A_WIDTH), _F32)],
        compiler_params=_params(("parallel", "arbitrary")),
        name="gmlp_front",
    )(xb, w_in_b, b_in, ln_g, ln_b, w_s, b_s_t)


def _proj_ln_kernel(*refs, has_bias):
    if has_bias:
        a_ref, w_ref, b_ref, xres_ref, g_ref, bb_ref, wr_ref, br_ref, h_ref, hb_ref, idx_ref, gate_ref = refs
    else:
        a_ref, w_ref, xres_ref, g_ref, bb_ref, wr_ref, br_ref, h_ref, hb_ref, idx_ref, gate_ref = refs
        b_ref = None
    wr = wr_ref[...]
    w_hi = wr.astype(_BF)
    w_lo = (wr - w_hi.astype(_F32)).astype(_BF)
    w_cat = jnp.concatenate([w_hi, w_lo], axis=1)
    rows = a_ref.shape[0] // PL_PARTS
    for part in range(PL_PARTS):
        rs = slice(part * rows, (part + 1) * rows)
        mix = jnp.dot(a_ref[rs, :], w_ref[...], preferred_element_type=_F32)
        if has_bias:
            mix = mix + b_ref[...]
        h = _ln(DEEPNORM_ALPHA * xres_ref[rs, :] + mix, g_ref[...], bb_ref[...])
        h_ref[rs, :] = h
        h_hi = h.astype(_BF)
        hb_ref[rs, :] = h_hi

        h_lo = (h - h_hi.astype(_F32)).astype(_BF)
        r = (jnp.dot(h_hi, w_cat, preferred_element_type=_F32)
             + jnp.dot(h_lo, w_cat, preferred_element_type=_F32))
        logits = r[:, :N_EXPERTS] + r[:, N_EXPERTS:] + br_ref[...]
        lane = lax.broadcasted_iota(jnp.int32, logits.shape, 1)
        cur = logits
        vals, idxs = [], []
        for _ in range(TOP_K):
            m = jnp.max(cur, axis=-1, keepdims=True)
            i = jnp.min(jnp.where(cur == m, lane, N_EXPERTS), axis=-1, keepdims=True)
            vals.append(m)
            idxs.append(i)
            cur = jnp.where(lane == i, -jnp.inf, cur)
        es = [jnp.exp(v - vals[0]) for v in vals]
        den = es[0] + es[1] + es[2] + es[3]
        colk = lax.broadcasted_iota(jnp.int32, (rows, TOP_K), 1)
        idx_out = jnp.zeros((rows, TOP_K), jnp.int32)
        gate_out = jnp.zeros((rows, TOP_K), _F32)
        for k in range(TOP_K):
            idx_out = jnp.where(colk == k, idxs[k], idx_out)
            gate_out = jnp.where(colk == k, es[k] / den, gate_out)
        idx_ref[rs, :] = idx_out
        gate_ref[rs, :] = gate_out


def _proj_ln(a_b, w_b, bias, xres, ln_g, ln_b, w_r, b_r):
    n, d = xres.shape
    tm = PL_TM
    has_bias = bias is not None
    row = lambda i: (i, 0)
    const = lambda i: (0, 0)
    in_specs = [pl.BlockSpec((tm, a_b.shape[1]), row), pl.BlockSpec(w_b.shape, const)]
    args = [a_b, w_b]
    if has_bias:
        in_specs.append(pl.BlockSpec((1, d), const))
        args.append(bias)
    in_specs += [pl.BlockSpec((tm, d), row), pl.BlockSpec((1, d), const), pl.BlockSpec((1, d), const),
                 pl.BlockSpec((d, N_EXPERTS), const), pl.BlockSpec((1, N_EXPERTS), const)]
    args += [xres, ln_g, ln_b, w_r, b_r]
    return pl.pallas_call(
        functools.partial(_proj_ln_kernel, has_bias=has_bias),
        grid=(n // tm,),
        in_specs=in_specs,
        out_specs=[pl.BlockSpec((tm, d), row), pl.BlockSpec((tm, d), row),
                   pl.BlockSpec((tm, TOP_K), row), pl.BlockSpec((tm, TOP_K), row)],
        out_shape=[jax.ShapeDtypeStruct((n, d), _F32), jax.ShapeDtypeStruct((n, d), _BF),
                   jax.ShapeDtypeStruct((n, TOP_K), jnp.int32), jax.ShapeDtypeStruct((n, TOP_K), _F32)],
        compiler_params=_params(("parallel",)),
        name="proj_ln",
    )(*args)


def _moe_layout(n_tok):
    n_slots = n_tok * TOP_K + N_EXPERTS * MOE_SUB
    n_groups = N_EXPERTS + (n_tok * TOP_K) // MOE_RMAX
    return n_slots, n_groups


def _slot_tokens_kernel(pos_ref, tok_ref):
    def init(i, c):
        tok_ref[i] = 0
        return c
    lax.fori_loop(0, tok_ref.shape[0], init, 0, unroll=8)

    def place(a, c):
        tok_ref[pos_ref[a]] = lax.shift_right_logical(a, TOP_K.bit_length() - 1)
        return c
    lax.fori_loop(0, pos_ref.shape[0], place, 0, unroll=8)


def _slot_tokens(pos_flat, n_slots):
    assert TOP_K & (TOP_K - 1) == 0
    return pl.pallas_call(
        _slot_tokens_kernel,
        in_specs=[pl.BlockSpec(memory_space=pltpu.SMEM)],
        out_specs=pl.BlockSpec(memory_space=pltpu.SMEM),
        out_shape=jax.ShapeDtypeStruct((n_slots,), jnp.int32),
        name="slot_tokens",
    )(pos_flat)


def _routing(idx):
    n = idx.shape[0]
    n_slots, n_groups = _moe_layout(n)
    e_ids = jnp.arange(N_EXPERTS, dtype=jnp.int32)
    onehot = (idx[:, :, None] == e_ids[None, None, :]).astype(jnp.int32).sum(axis=1)
    incl = jnp.cumsum(onehot, axis=0)
    counts = incl[-1]
    rank = jnp.take_along_axis(incl - onehot, idx, axis=1)
    padded = (counts + MOE_SUB - 1) // MOE_SUB * MOE_SUB
    pstart = jnp.cumsum(padded) - padded
    pos = (pstart[idx] + rank).astype(jnp.int32)
    slot_tok = _slot_tokens(pos.reshape(-1), n_slots)
    ng = (counts + MOE_RMAX - 1) // MOE_RMAX
    gend = jnp.cumsum(ng)
    gstart = gend - ng
    total = gend[-1]
    gi = jnp.arange(n_groups, dtype=jnp.int32)
    gi_c = jnp.minimum(gi, total - 1)
    ge = jnp.searchsorted(gend, gi_c, side='right').astype(jnp.int32)
    local = gi_c - gstart[ge]
    gs = (pstart[ge] + local * MOE_RMAX).astype(jnp.int32)
    rows = jnp.clip(counts[ge] - local * MOE_RMAX, 0, MOE_RMAX)
    nsub = jnp.where(gi < total, (rows + MOE_SUB - 1) // MOE_SUB, 0).astype(jnp.int32)
    tail = jnp.sum(padded).astype(jnp.int32).reshape(1)
    return pos, slot_tok, ge, gs, nsub, tail


def _moe_kernel(ge_ref, gs_ref, nsub_ref, tail_ref, tok_ref,
                h_hbm, wg_ref, wu_ref, wd_ref, bg_ref, bu_ref, bd_ref,
                y_hbm,
                xg, xb, yacc, wgb, wub, wdb, issued, sem_g, sem_o):
    g = pl.program_id(0)
    j = pl.program_id(1)
    n_grp = pl.num_programs(0)
    ns = nsub_ref[g]
    g_next = jnp.minimum(g + 1, n_grp - 1)
    next_rows = jnp.where(g + 1 < n_grp, nsub_ref[g_next], 0) * MOE_SUB
    next_base = gs_ref[g_next]

    def gather_copy(t, i8, u):
        return pltpu.make_async_copy(h_hbm.at[pl.ds(t, 1)], xg.at[i8, pl.ds(u, 1)], sem_g)

    def issue8(slot_base, t8):
        for u in range(8):
            gather_copy(tok_ref[slot_base + t8 * 8 + u], t8, u).start()

    def out_copy(base, r0):
        dst = pl.multiple_of(base + r0, MOE_SUB)
        return pltpu.make_async_copy(yacc.at[pl.ds(r0, MOE_SUB)], y_hbm.at[pl.ds(dst, MOE_SUB)], sem_o)

    @pl.when((g == 0) & (j == 0))
    def _():
        issued[0] = 0

    @pl.when(ns > 0)
    def _():
        base = pl.multiple_of(gs_ref[g], MOE_SUB)

        @pl.when(j == 0)
        def _():
            def issue(t8, c):
                issue8(base, t8)
                return c
            lax.fori_loop(issued[0] // 8, ns * (MOE_SUB // 8), issue, 0)
            issued[0] = 0

            def wait_row(i, c):
                gather_copy(0, 0, 0).wait()
                return c

            def wait_sub(s, c):
                return lax.fori_loop(0, MOE_SUB, wait_row, c, unroll=8)
            lax.fori_loop(0, ns, wait_sub, 0)

            bias_rows = jnp.broadcast_to(bd_ref[0], (MOE_SUB, yacc.shape[1]))

            def cast(s, c):
                r0 = pl.multiple_of(s * MOE_SUB, MOE_SUB)
                t0 = s * (MOE_SUB // 8)
                xs = xg[pl.ds(t0, MOE_SUB // 8)].reshape(MOE_SUB, xb.shape[1])
                xb[pl.ds(r0, MOE_SUB), :] = xs.astype(_BF)
                yacc[pl.ds(r0, MOE_SUB), :] = bias_rows
                return c
            lax.fori_loop(0, ns, cast, 0)

        wgb[...] = wg_ref[0].astype(_BF)
        wub[...] = wu_ref[0].astype(_BF)
        wdb[...] = wd_ref[0].astype(_BF)

        def compute(r0):
            xs = xb[pl.ds(r0, MOE_SUB), :]
            gg = jnp.dot(xs, wgb[...], preferred_element_type=_F32) + bg_ref[0]
            uu = jnp.dot(xs, wub[...], preferred_element_type=_F32) + bu_ref[0]
            gg = jnp.minimum(gg, SWIGLU_LIMIT)
            uu = jnp.clip(uu, -SWIGLU_LIMIT, SWIGLU_LIMIT)
            act = (uu + 1.0) * (gg * jax.nn.sigmoid(SWIGLU_ALPHA * gg))
            yacc[pl.ds(r0, MOE_SUB), :] += jnp.dot(act.astype(_BF), wdb[...], preferred_element_type=_F32)

        def step(s0, n_sub):
            def body(prefetch):
                if prefetch:
                    t8_0 = issued[0] // 8
                    for i8 in range(MOE_SUB // 8):
                        issue8(next_base, t8_0 + i8)
                    issued[0] = issued[0] + MOE_SUB
                for k in range(n_sub):
                    compute(pl.multiple_of((s0 + k) * MOE_SUB, MOE_SUB))

            lax.cond(issued[0] < next_rows, lambda: body(True), lambda: body(False))

            @pl.when(j == MOE_J - 1)
            def _():
                for k in range(n_sub):
                    out_copy(base, pl.multiple_of((s0 + k) * MOE_SUB, MOE_SUB)).start()

        def pair(p, c):
            step(2 * p, 2)
            return c
        lax.fori_loop(0, ns // 2, pair, 0)

        @pl.when(ns % 2 == 1)
        def _():
            step(ns - 1, 1)

        @pl.when(j == MOE_J - 1)
        def _():
            def drain(s, c):
                out_copy(base, pl.multiple_of(s * MOE_SUB, MOE_SUB)).wait()
                return c
            lax.fori_loop(0, ns, drain, 0)

    @pl.when((g == pl.num_programs(0) - 1) & (j == MOE_J - 1))
    def _():
        tail = pl.multiple_of(tail_ref[0], MOE_SUB)
        n_tail = (y_hbm.shape[0] - tail) // MOE_SUB
        yacc[pl.ds(0, MOE_SUB), :] = jnp.zeros((MOE_SUB, yacc.shape[1]), _F32)

        def tail_copy(s):
            dst = pl.multiple_of(tail + s * MOE_SUB, MOE_SUB)
            return pltpu.make_async_copy(yacc.at[pl.ds(0, MOE_SUB)], y_hbm.at[pl.ds(dst, MOE_SUB)], sem_o)

        def start(s, c):
            tail_copy(s).start()
            return c
        lax.fori_loop(0, n_tail, start, 0)

        def done(s, c):
            tail_copy(s).wait()
            return c
        lax.fori_loop(0, n_tail, done, 0)


def _moe_experts(h, w_gu, b_gu, w_down, b_down, layer, slot_tok, ge, gs, nsub, tail):
    n, d = h.shape
    n_rows = w_gu.shape[0] * N_EXPERTS
    w_gu = w_gu.reshape(n_rows, d, 2 * EXPERT_FF)
    w_down = w_down.reshape(n_rows, EXPERT_FF, d)
    b_gu = b_gu.reshape(n_rows, 1, 2 * EXPERT_FF)
    b_down = b_down.reshape(n_rows, 1, d)
    ge = ge + layer * N_EXPERTS
    n_slots, n_groups = _moe_layout(n)
    f, jn = MOE_F, MOE_J

    def jj(g, j, nsub_ref):
        return jnp.where(nsub_ref[g] > 0, j, jn - 1)

    grid_spec = pltpu.PrefetchScalarGridSpec(
        num_scalar_prefetch=5,
        grid=(n_groups, jn),
        in_specs=[
            pl.BlockSpec(memory_space=pl.ANY),
            pl.BlockSpec((1, d, f), lambda g, j, ge, gs, ns, tl, tk: (ge[g], 0, jj(g, j, ns))),
            pl.BlockSpec((1, d, f), lambda g, j, ge, gs, ns, tl, tk: (ge[g], 0, jn + jj(g, j, ns))),
            pl.BlockSpec((1, f, d), lambda g, j, ge, gs, ns, tl, tk: (ge[g], jj(g, j, ns), 0)),
            pl.BlockSpec((1, 1, f), lambda g, j, ge, gs, ns, tl, tk: (ge[g], 0, jj(g, j, ns))),
            pl.BlockSpec((1, 1, f), lambda g, j, ge, gs, ns, tl, tk: (ge[g], 0, jn + jj(g, j, ns))),
            pl.BlockSpec((1, 1, d), lambda g, j, ge, gs, ns, tl, tk: (ge[g], 0, 0)),
        ],
        out_specs=pl.BlockSpec(memory_space=pl.ANY),
        scratch_shapes=[
            pltpu.VMEM((MOE_RMAX // 8, 8, d), _F32),
            pltpu.VMEM((MOE_RMAX, d), _BF),
            pltpu.VMEM((MOE_RMAX, d), _F32),
            pltpu.VMEM((d, f), _BF),
            pltpu.VMEM((d, f), _BF),
            pltpu.VMEM((f, d), _BF),
            pltpu.SMEM((1,), jnp.int32),
            pltpu.SemaphoreType.DMA(()),
            pltpu.SemaphoreType.DMA(()),
        ],
    )
    return pl.pallas_call(
        _moe_kernel,
        grid_spec=grid_spec,
        out_shape=jax.ShapeDtypeStruct((n_slots, d), _F32),
        compiler_params=_params(("arbitrary", "arbitrary")),
        name="moe_experts",
    )(ge, gs, nsub, tail, slot_tok, h, w_gu, w_gu, w_down, b_gu, b_gu, b_down)


def _combine_ln2_kernel(pos_ref, h_ref, hb_ref, p_ref, gate_ref, wg_ref, bg_ref, wp_ref, g_ref, b_ref, y_hbm,
                        x_ref, xb_ref, yg, sem, *, tm):
    i = pl.program_id(0)
    base = i * (tm * TOP_K)

    def issue(r8):
        for u in range(8):
            for k in range(TOP_K):
                s = pos_ref[base + (r8 * 8 + u) * TOP_K + k]
                pltpu.make_async_copy(y_hbm.at[pl.ds(s, 1)], yg.at[k, r8, pl.ds(u, 1)], sem).start()

    d = h_ref.shape[1]
    n_ch = C2_CHUNKS
    cw = d // n_ch
    hb = hb_ref[...]
    pb = p_ref[...].astype(_BF)
    parts = []
    for c in range(n_ch):
        for r8 in range(c * (tm // 8 // n_ch), (c + 1) * (tm // 8 // n_ch)):
            issue(r8)
        cs = slice(c * cw, (c + 1) * cw)
        gate_v = jax.nn.sigmoid(jnp.dot(hb, wg_ref[:, cs], preferred_element_type=_F32) + bg_ref[:, cs])
        pp = jnp.dot(pb, wp_ref[:, cs], preferred_element_type=_F32)
        parts.append(DEEPNORM_ALPHA * h_ref[:, cs] + gate_v * pp)
    acc = jnp.concatenate(parts, axis=1)

    def wait_row(r, c):
        pltpu.make_async_copy(y_hbm.at[pl.ds(0, 1)], yg.at[0, 0, pl.ds(0, 1)], sem).wait()
        return c
    lax.fori_loop(0, tm * TOP_K, wait_row, 0, unroll=8)

    gates = gate_ref[...]
    for k in range(TOP_K):
        acc = acc + gates[:, k:k + 1] * yg[k].reshape(acc.shape)
    xn = _ln(acc, g_ref[...], b_ref[...])
    x_ref[...] = xn
    xb_ref[...] = xn.astype(_BF)


def _combine_ln2(h, hb, p_i, gates, pos, y_sorted, w_g_b, b_g, w_p_b, ln_g, ln_b):
    n, d = h.shape
    tm = C2_TM
    row = lambda i, pos_ref: (i, 0)
    const = lambda i, pos_ref: (0, 0)
    grid_spec = pltpu.PrefetchScalarGridSpec(
        num_scalar_prefetch=1,
        grid=(n // tm,),
        in_specs=[
            pl.BlockSpec((tm, d), row), pl.BlockSpec((tm, d), row), pl.BlockSpec((tm, PLE_DIM), row),
            pl.BlockSpec((tm, TOP_K), row),
            pl.BlockSpec((d, d), const), pl.BlockSpec((1, d), const), pl.BlockSpec((PLE_DIM, d), const),
            pl.BlockSpec((1, d), const), pl.BlockSpec((1, d), const),
            pl.BlockSpec(memory_space=pl.ANY),
        ],
        out_specs=[pl.BlockSpec((tm, d), row), pl.BlockSpec((tm, d), row)],
        scratch_shapes=[pltpu.VMEM((TOP_K, tm // 8, 8, d), _F32), pltpu.SemaphoreType.DMA(())],
    )
    return pl.pallas_call(
        functools.partial(_combine_ln2_kernel, tm=tm),
        grid_spec=grid_spec,
        out_shape=[jax.ShapeDtypeStruct((n, d), _F32), jax.ShapeDtypeStruct((n, d), _BF)],
        compiler_params=_params(("arbitrary",)),
        name="combine_ln2",
    )(pos.reshape(-1), h, hb, p_i, gates, w_g_b, b_g, w_p_b, ln_g, ln_b, y_sorted)


def _mm_kernel(a_ref, w_ref, o_ref):
    o_ref[...] = jnp.dot(a_ref[...], w_ref[...], preferred_element_type=_F32)


def _mm(a_b, w_b):
    n, d = a_b.shape
    nout = w_b.shape[1]
    tm, tn = MM_TM, MM_TN
    return pl.pallas_call(
        _mm_kernel,
        grid=(n // tm, nout // tn),
        in_specs=[pl.BlockSpec((tm, d), lambda i, j: (i, 0)), pl.BlockSpec((d, tn), lambda i, j: (0, j))],
        out_specs=pl.BlockSpec((tm, tn), lambda i, j: (i, j)),
        out_shape=jax.ShapeDtypeStruct((n, nout), _F32),
        compiler_params=_params(("parallel", "arbitrary")),
        name="mm",
    )(a_b, w_b)


def _attn_kernel(slopes_ref, q0_ref, q1_ref, q2_ref, kc_ref, kp_ref, vc_ref, vp_ref, o_ref,
                 bias_s, m_s, l_s, acc_s):
    j = pl.program_id(1)
    h = pl.program_id(2)
    neg_slope = -slopes_ref[h]
    qi = lax.broadcasted_iota(jnp.int32, (QBLK, 2 * QBLK), 0)
    ki = lax.broadcasted_iota(jnp.int32, (QBLK, 2 * QBLK), 1)
    dist = qi + QBLK - ki
    n_groups = B_GROUPS
    for g, d in enumerate(B_DILATIONS):
        n_back = B_WINDOWS[g] // d
        valid = (dist >= 0) & (dist <= n_back)
        bias = jnp.where(valid, neg_slope * (dist * d).astype(_F32), NEG_INF)
        bias_s[g] = bias
        bias_s[n_groups + g] = jnp.where(j == 0, jnp.where(ki < QBLK, NEG_INF, bias), bias)

    scale = np.float32(HEAD_DIM ** -0.5)

    def block(g, d, q_ref, start, kband, vband, bias):
        q = q_ref[pl.ds(start, QBLK, stride=d), :].astype(_BF) if d > 1 else q_ref[pl.ds(start, QBLK), :].astype(_BF)
        s = lax.dot_general(q, kband, (((1,), (1,)), ((), ())), preferred_element_type=_F32)
        s = s * scale + bias
        m = jnp.max(s, axis=-1, keepdims=True)
        p = jnp.exp(s - m)
        l = jnp.sum(p, axis=-1, keepdims=True)
        pv = jnp.dot(p.astype(_BF), vband, preferred_element_type=_F32)
        mb = jnp.broadcast_to(m, (QBLK, HEAD_DIM))
        lb = jnp.broadcast_to(l, (QBLK, HEAD_DIM))
        if d > 1:
            idx = (g, pl.ds(start, QBLK, stride=d), slice(None))
        else:
            idx = (g, pl.ds(start, QBLK), slice(None))
        m_s[idx] = mb
        l_s[idx] = lb
        acc_s[idx] = pv

    def rows(ref, start, size, d):
        if d > 1:
            return ref[pl.ds(start, size, stride=d), :].astype(_BF)
        return ref[pl.ds(start, size), :].astype(_BF)

    for g, d in enumerate(B_DILATIONS):
        q_ref = (q0_ref, q1_ref, q2_ref)[g]
        span = QBLK * d
        nloc = ATT_T // span

        def first_block(r, g=g, d=d, q_ref=q_ref, span=span, nloc=nloc):
            pstart = (nloc - 1) * span + r
            kband = jnp.concatenate([rows(kp_ref, pstart, QBLK, d), rows(kc_ref, r, QBLK, d)], axis=0)
            vband = jnp.concatenate([rows(vp_ref, pstart, QBLK, d), rows(vc_ref, r, QBLK, d)], axis=0)
            block(g, d, q_ref, r, kband, vband, bias_s[n_groups + g])

        def later_block(r, n, g=g, d=d, q_ref=q_ref, span=span):
            bstart = (n - 1) * span + r
            kband = rows(kc_ref, bstart, 2 * QBLK, d)
            vband = rows(vc_ref, bstart, 2 * QBLK, d)
            block(g, d, q_ref, n * span + r, kband, vband, bias_s[g])

        if nloc == 1:
            def body(r, c, first_block=first_block):
                first_block(r)
                return c
            lax.fori_loop(0, d, body, 0, unroll=8)
        elif d == 1:
            first_block(0)

            def body(n, c, later_block=later_block):
                later_block(0, n)
                return c
            lax.fori_loop(1, nloc, body, 0, unroll=5)
        else:
            def body(r, c, first_block=first_block, later_block=later_block, nloc=nloc):
                first_block(r)
                for n in range(1, nloc):
                    later_block(r, n)
                return c
            lax.fori_loop(0, d, body, 0, unroll=2)

    m0, m1, m2 = m_s[0], m_s[1], m_s[2]
    mm = jnp.maximum(jnp.maximum(m0, m1), m2)
    w0 = jnp.exp(m0 - mm)
    w1 = jnp.exp(m1 - mm)
    w2 = jnp.exp(m2 - mm)
    num = w0 * acc_s[0] + w1 * acc_s[1] + w2 * acc_s[2]
    den = w0 * l_s[0] + w1 * l_s[1] + w2 * l_s[2]
    o_ref[...] = (num / den).astype(o_ref.dtype)


def _dilated_attention(q, kv, bsz, seq):
    n = q.shape[0]
    t = ATT_T
    tiles = seq // t
    hd = B_HEADS
    slopes = 2.0 ** (-8.0 * jnp.arange(1, hd + 1, dtype=_F32) / hd)

    def qspec(g):
        return pl.BlockSpec((t, HEAD_DIM), lambda b, j, h, s, g=g: (b * tiles + j, g * hd + h))

    def cur(off):
        return pl.BlockSpec((t, HEAD_DIM), lambda b, j, h, s: (b * tiles + j, off + h))

    def prev(off):
        return pl.BlockSpec((t, HEAD_DIM), lambda b, j, h, s: (b * tiles + jnp.maximum(j - 1, 0), off + h))

    grid_spec = pltpu.PrefetchScalarGridSpec(
        num_scalar_prefetch=1,
        grid=(bsz, tiles, hd),
        in_specs=[qspec(0), qspec(1), qspec(2), cur(0), prev(0), cur(hd), prev(hd)],
        out_specs=pl.BlockSpec((t, HEAD_DIM), lambda b, j, h, s: (b * tiles + j, h)),
        scratch_shapes=[
            pltpu.VMEM((2 * B_GROUPS, QBLK, 2 * QBLK), _F32),
            pltpu.VMEM((B_GROUPS, t, HEAD_DIM), _F32),
            pltpu.VMEM((B_GROUPS, t, HEAD_DIM), _F32),
            pltpu.VMEM((B_GROUPS, t, HEAD_DIM), _F32),
        ],
    )
    return pl.pallas_call(
        _attn_kernel,
        grid_spec=grid_spec,
        out_shape=jax.ShapeDtypeStruct((n, hd * HEAD_DIM), _BF),
        compiler_params=_params(("parallel", "parallel", "arbitrary")),
        name="dilated_attn",
    )(slopes, q, q, q, kv, kv, kv, kv)


def kernel(x, p, a_w_in, a_b_in, a_ln_g, a_ln_b, a_w_s, a_b_s, a_w_out, a_b_out, kv_w, b_w_q, b_w_o, ln1_g, ln1_b, ln2_g, ln2_b, router_w, router_b, moe_w_gu, moe_b_gu, moe_w_down, moe_b_down, ple_w_p, ple_w_g, ple_b_g):
    bsz, seq, d = x.shape
    n = bsz * seq
    xf = x.reshape(n, d)
    xb = xf.astype(_BF)
    row = lambda v: v.reshape(1, -1)
    kv = None
    for i in range(DEPTH):
        if i < N_A_LAYERS:
            gated = _gmlp_front(xb, a_w_in[i].astype(_BF), row(a_b_in[i]), row(a_ln_g[i]), row(a_ln_b[i]),
                                a_w_s[i], a_b_s[i].T)
            h, hb, idx, gates = _proj_ln(gated, a_w_out[i].astype(_BF), row(a_b_out[i]), xf,
                                         row(ln1_g[i]), row(ln1_b[i]), router_w[i], row(router_b[i]))
        else:
            jl = i - N_A_LAYERS
            q = _mm(xb, b_w_q[jl].astype(_BF))
            att = _dilated_attention(q, kv, bsz, seq)
            h, hb, idx, gates = _proj_ln(att, b_w_o[jl].astype(_BF), None, xf,
                                         row(ln1_g[i]), row(ln1_b[i]), router_w[i], row(router_b[i]))
        pos, slot_tok, ge, gs, nsub, tail = _routing(idx)
        y_sorted = _moe_experts(h, moe_w_gu, moe_b_gu, moe_w_down, moe_b_down, i,
                                slot_tok, ge, gs, nsub, tail)
        xf, xb = _combine_ln2(h, hb, p[i].reshape(n, PLE_DIM), gates, pos, y_sorted,
                              ple_w_g[i].astype(_BF), row(ple_b_g[i]), ple_w_p[i].astype(_BF),
                              row(ln2_g[i]), row(ln2_b[i]))
        if i == N_A_LAYERS - 1:
            kv = _mm(xb, kv_w.astype(_BF))
    return xf.reshape(bsz, seq, d)
```

```python
import functools

import jax
import jax.numpy as jnp
import numpy as np
from jax import lax
from jax.experimental import pallas as pl
from jax.experimental.pallas import tpu as pltpu

D_MODEL = 2048
DEPTH = 4
N_A_LAYERS = DEPTH // 2
CHUNK = 128
A_WIDTH = D_MODEL
A_GROUPS = 8
A_GROUP_CH = A_WIDTH // A_GROUPS
HEAD_DIM = 128
B_HEADS = D_MODEL // HEAD_DIM
B_WINDOWS = (128, 512, 2048)
B_DILATIONS = (1, 4, 16)
B_GROUPS = len(B_WINDOWS)
QBLK = 128
N_EXPERTS = 32
TOP_K = 4
EXPERT_FF = D_MODEL // 2
SWIGLU_LIMIT = 7.0
SWIGLU_ALPHA = 1.702
PLE_DIM = 256
LN_EPS = 1e-5
DEEPNORM_ALPHA = (2.0 * DEPTH) ** 0.25
NEG_INF = -1e30

V7X_VMEM_BYTES = 64 * 1024 * 1024
VMEM_LIMIT = 56 * 1024 * 1024
GF_TM, GF_TK = 512, 512
PL_TM = 512
PL_PARTS = 2
MM_TM, MM_TN = 1024, 512
ATT_T = QBLK * max(B_DILATIONS)
MOE_F = 256
MOE_J = EXPERT_FF // MOE_F
MOE_SUB = 256
MOE_RMAX = 1280
C2_TM = 256
C2_CHUNKS = 4

_BF = jnp.bfloat16
_F32 = jnp.float32


def _params(sem):
    return pltpu.CompilerParams(dimension_semantics=sem, vmem_limit_bytes=VMEM_LIMIT)


def _ln(y, g, b):
    mu = jnp.mean(y, axis=-1, keepdims=True)
    yc = y - mu
    var = jnp.mean(yc * yc, axis=-1, keepdims=True)
    return yc * lax.rsqrt(var + LN_EPS) * g + b


def _gmlp_front_kernel(x_ref, w_ref, b_ref, lng_ref, lnb_ref, ws_ref, bst_ref, o_ref, acc_ref, *, nk, tm):
    k = pl.program_id(1)

    @pl.when(k == 0)
    def _():
        acc_ref[...] = jnp.dot(x_ref[...], w_ref[...], preferred_element_type=_F32)

    @pl.when(k > 0)
    def _():
        acc_ref[...] += jnp.dot(x_ref[...], w_ref[...], preferred_element_type=_F32)

    @pl.when(k == nk - 1)
    def _():
        row = lax.broadcasted_iota(jnp.int32, (CHUNK, CHUNK), 0)
        col = lax.broadcasted_iota(jnp.int32, (CHUNK, CHUNK), 1)
        causal = row >= col
        sqrt_half = np.float32(np.sqrt(0.5))

        def chunk(c, carry):
            r0 = pl.multiple_of(c * CHUNK, CHUNK)
            z = acc_ref[pl.ds(r0, CHUNK), :] + b_ref[...]
            z = 0.5 * z * (1.0 + lax.erf(z * sqrt_half))
            u = z[:, :A_WIDTH]
            v = _ln(z[:, A_WIDTH:], lng_ref[...], lnb_ref[...]).astype(_BF)
            for g in range(A_GROUPS):
                wg = jnp.where(causal, ws_ref[g], 0.0).astype(_BF)
                cs = slice(g * A_GROUP_CH, (g + 1) * A_GROUP_CH)
                vs = jnp.dot(wg, v[:, cs], preferred_element_type=_F32) + bst_ref[:, g:g + 1]
                o_ref[pl.ds(r0, CHUNK), cs] = (u[:, cs] * vs).astype(_BF)
            return carry

        lax.fori_loop(0, tm // CHUNK, chunk, 0)


def _gmlp_front(xb, w_in_b, b_in, ln_g, ln_b, w_s, b_s_t):
    n, d = xb.shape
    tm, tk = GF_TM, GF_TK
    nk = d // tk
    return pl.pallas_call(
        functools.partial(_gmlp_front_kernel, nk=nk, tm=tm),
        grid=(n // tm, nk),
        in_specs=[
            pl.BlockSpec((tm, tk), lambda i, k: (i, k)),
            pl.BlockSpec((tk, 2 * A_WIDTH), lambda i, k: (k, 0)),
            pl.BlockSpec((1, 2 * A_WIDTH), lambda i, k: (0, 0)),
            pl.BlockSpec((1, A_WIDTH), lambda i, k: (0, 0)),
            pl.BlockSpec((1, A_WIDTH), lambda i, k: (0, 0)),
            pl.BlockSpec((A_GROUPS, CHUNK, CHUNK), lambda i, k: (0, 0, 0)),
            pl.BlockSpec((CHUNK, A_GROUPS), lambda i, k: (0, 0)),
        ],
        out_specs=pl.BlockSpec((tm, A_WIDTH), lambda i, k: (i, 0)),
        out_shape=jax.ShapeDtypeStruct((n, A_WIDTH), _BF),
        scratch_shapes=[pltpu.VMEM((tm, 2 * A_WIDTH), _F32)],
        compiler_params=_params(("parallel", "arbitrary")),
        name="gmlp_front",
    )(xb, w_in_b, b_in, ln_g, ln_b, w_s, b_s_t)


def _proj_ln_kernel(*refs, has_bias):
    if has_bias:
        a_ref, w_ref, b_ref, xres_ref, g_ref, bb_ref, wr_ref, br_ref = refs[:8]
    else:
        a_ref, w_ref, xres_ref, g_ref, bb_ref, wr_ref, br_ref = refs[:7]
        b_ref = None
    h_ref, hb_ref, idx_ref, gate_ref, rank_ref, cnt_ref = refs[-6:]
    wr = wr_ref[...]
    w_hi = wr.astype(_BF)
    w_lo = (wr - w_hi.astype(_F32)).astype(_BF)
    w_cat = jnp.concatenate([w_hi, w_lo], axis=1)
    rows = a_ref.shape[0] // PL_PARTS

    @pl.when(pl.program_id(0) == 0)
    def _():
        cnt_ref[...] = jnp.zeros_like(cnt_ref)
    earlier = (lax.broadcasted_iota(jnp.int32, (rows, rows), 0)
               > lax.broadcasted_iota(jnp.int32, (rows, rows), 1)).astype(_BF)
    for part in range(PL_PARTS):
        rs = slice(part * rows, (part + 1) * rows)
        mix = jnp.dot(a_ref[rs, :], w_ref[...], preferred_element_type=_F32)
        if has_bias:
            mix = mix + b_ref[...]
        h = _ln(DEEPNORM_ALPHA * xres_ref[rs, :] + mix, g_ref[...], bb_ref[...])
        h_ref[rs, :] = h
        h_hi = h.astype(_BF)
        hb_ref[rs, :] = h_hi

        h_lo = (h - h_hi.astype(_F32)).astype(_BF)
        r = (jnp.dot(h_hi, w_cat, preferred_element_type=_F32)
             + jnp.dot(h_lo, w_cat, preferred_element_type=_F32))
        logits = r[:, :N_EXPERTS] + r[:, N_EXPERTS:] + br_ref[...]
        lane = lax.broadcasted_iota(jnp.int32, logits.shape, 1)
        cur = logits
        vals, idxs = [], []
        for _ in range(TOP_K):
            m = jnp.max(cur, axis=-1, keepdims=True)
            i = jnp.min(jnp.where(cur == m, lane, N_EXPERTS), axis=-1, keepdims=True)
            vals.append(m)
            idxs.append(i)
            cur = jnp.where(lane == i, -jnp.inf, cur)
        es = [jnp.exp(v - vals[0]) for v in vals]
        den = es[0] + es[1] + es[2] + es[3]
        colk = lax.broadcasted_iota(jnp.int32, (rows, TOP_K), 1)
        idx_out = jnp.zeros((rows, TOP_K), jnp.int32)
        gate_out = jnp.zeros((rows, TOP_K), _F32)
        for k in range(TOP_K):
            idx_out = jnp.where(colk == k, idxs[k], idx_out)
            gate_out = jnp.where(colk == k, es[k] / den, gate_out)
        idx_ref[rs, :] = idx_out
        gate_ref[rs, :] = gate_out

        hot = [lane == idxs[k] for k in range(TOP_K)]
        member = (hot[0] | hot[1] | hot[2] | hot[3]).astype(_F32)
        before = jnp.dot(earlier, member.astype(_BF), preferred_element_type=_F32) + cnt_ref[...]
        rank_out = jnp.zeros((rows, TOP_K), _F32)
        for k in range(TOP_K):
            rank_k = jnp.sum(jnp.where(hot[k], before, 0.0), axis=-1, keepdims=True)
            rank_out = jnp.where(colk == k, rank_k, rank_out)
        rank_ref[rs, :] = rank_out.astype(jnp.int32)
        cnt_ref[...] = cnt_ref[...] + jnp.sum(member, axis=0, keepdims=True)


def _proj_ln(a_b, w_b, bias, xres, ln_g, ln_b, w_r, b_r):
    n, d = xres.shape
    tm = PL_TM
    has_bias = bias is not None
    row = lambda i: (i, 0)
    const = lambda i: (0, 0)
    in_specs = [pl.BlockSpec((tm, a_b.shape[1]), row), pl.BlockSpec(w_b.shape, const)]
    args = [a_b, w_b]
    if has_bias:
        in_specs.append(pl.BlockSpec((1, d), const))
        args.append(bias)
    in_specs += [pl.BlockSpec((tm, d), row), pl.BlockSpec((1, d), const), pl.BlockSpec((1, d), const),
                 pl.BlockSpec((d, N_EXPERTS), const), pl.BlockSpec((1, N_EXPERTS), const)]
    args += [xres, ln_g, ln_b, w_r, b_r]
    return pl.pallas_call(
        functools.partial(_proj_ln_kernel, has_bias=has_bias),
        grid=(n // tm,),
        in_specs=in_specs,
        out_specs=[pl.BlockSpec((tm, d), row), pl.BlockSpec((tm, d), row),
                   pl.BlockSpec((tm, TOP_K), row), pl.BlockSpec((tm, TOP_K), row),
                   pl.BlockSpec((tm, TOP_K), row), pl.BlockSpec((1, N_EXPERTS), const)],
        out_shape=[jax.ShapeDtypeStruct((n, d), _F32), jax.ShapeDtypeStruct((n, d), _BF),
                   jax.ShapeDtypeStruct((n, TOP_K), jnp.int32), jax.ShapeDtypeStruct((n, TOP_K), _F32),
                   jax.ShapeDtypeStruct((n, TOP_K), jnp.int32), jax.ShapeDtypeStruct((1, N_EXPERTS), _F32)],
        compiler_params=_params(("arbitrary",)),
        name="proj_ln",
    )(*args)


def _moe_layout(n_tok):
    n_slots = n_tok * TOP_K + N_EXPERTS * MOE_SUB
    n_groups = N_EXPERTS + (n_tok * TOP_K) // MOE_RMAX
    return n_slots, n_groups


def _slot_tokens_kernel(pos_ref, tok_ref, zeros_ref, sem):
    zeros_ref[...] = jnp.zeros_like(zeros_ref)
    fill = pltpu.make_async_copy(zeros_ref, tok_ref, sem)
    fill.start()
    fill.wait()

    def place(a, c):
        tok_ref[pos_ref[a]] = lax.shift_right_logical(a, TOP_K.bit_length() - 1)
        return c
    lax.fori_loop(0, pos_ref.shape[0], place, 0, unroll=8)


def _slot_tokens(pos_flat, n_slots):
    assert TOP_K & (TOP_K - 1) == 0
    return pl.pallas_call(
        _slot_tokens_kernel,
        in_specs=[pl.BlockSpec(memory_space=pltpu.SMEM)],
        out_specs=pl.BlockSpec(memory_space=pltpu.SMEM),
        out_shape=jax.ShapeDtypeStruct((n_slots,), jnp.int32),
        scratch_shapes=[pltpu.VMEM((n_slots,), jnp.int32), pltpu.SemaphoreType.DMA(())],
        name="slot_tokens",
    )(pos_flat)


def _routing(idx, rank, counts):
    n = idx.shape[0]
    n_slots, n_groups = _moe_layout(n)
    counts = counts.reshape(N_EXPERTS).astype(jnp.int32)
    padded = (counts + MOE_SUB - 1) // MOE_SUB * MOE_SUB
    pstart = jnp.cumsum(padded) - padded
    pos = (pstart[idx] + rank).astype(jnp.int32)
    slot_tok = _slot_tokens(pos.reshape(-1), n_slots)
    ng = (counts + MOE_RMAX - 1) // MOE_RMAX
    gend = jnp.cumsum(ng)
    gstart = gend - ng
    total = gend[-1]
    gi = jnp.arange(n_groups, dtype=jnp.int32)
    gi_c = jnp.minimum(gi, total - 1)
    ge = jnp.searchsorted(gend, gi_c, side='right').astype(jnp.int32)
    local = gi_c - gstart[ge]
    gs = (pstart[ge] + local * MOE_RMAX).astype(jnp.int32)
    rows = jnp.clip(counts[ge] - local * MOE_RMAX, 0, MOE_RMAX)
    nsub = jnp.where(gi < total, (rows + MOE_SUB - 1) // MOE_SUB, 0).astype(jnp.int32)
    tail = jnp.sum(padded).astype(jnp.int32).reshape(1)
    return pos, slot_tok, ge, gs, nsub, tail


def _moe_kernel(ge_ref, gs_ref, nsub_ref, tail_ref, tok_ref,
                h_hbm, wg_ref, wu_ref, wd_ref, bg_ref, bu_ref, bd_ref,
                y_hbm,
                xg, xb, yacc, wgb, wub, wdb, issued, sem_g, sem_o):
    g = pl.program_id(0)
    j = pl.program_id(1)
    n_grp = pl.num_programs(0)
    ns = nsub_ref[g]
    g_next = jnp.minimum(g + 1, n_grp - 1)
    next_rows = jnp.where(g + 1 < n_grp, nsub_ref[g_next], 0) * MOE_SUB
    next_base = gs_ref[g_next]

    def gather_copy(t, i8, u):
        return pltpu.make_async_copy(h_hbm.at[pl.ds(t, 1)], xg.at[i8, pl.ds(u, 1)], sem_g)

    def issue8(slot_base, t8):
        for u in range(8):
            gather_copy(tok_ref[slot_base + t8 * 8 + u], t8, u).start()

    def out_copy(base, r0):
        dst = pl.multiple_of(base + r0, MOE_SUB)
        return pltpu.make_async_copy(yacc.at[pl.ds(r0, MOE_SUB)], y_hbm.at[pl.ds(dst, MOE_SUB)], sem_o)

    @pl.when((g == 0) & (j == 0))
    def _():
        issued[0] = 0

    @pl.when(ns > 0)
    def _():
        base = pl.multiple_of(gs_ref[g], MOE_SUB)

        @pl.when(j == 0)
        def _():
            def issue(t8, c):
                issue8(base, t8)
                return c
            lax.fori_loop(issued[0] // 8, ns * (MOE_SUB // 8), issue, 0)
            issued[0] = 0

            def wait_row(i, c):
                gather_copy(0, 0, 0).wait()
                return c

            def wait_sub(s, c):
                return lax.fori_loop(0, MOE_SUB, wait_row, c, unroll=8)
            lax.fori_loop(0, ns, wait_sub, 0)

            bias_rows = jnp.broadcast_to(bd_ref[0], (MOE_SUB, yacc.shape[1]))

            def cast(s, c):
                r0 = pl.multiple_of(s * MOE_SUB, MOE_SUB)
                t0 = s * (MOE_SUB // 8)
                xs = xg[pl.ds(t0, MOE_SUB // 8)].reshape(MOE_SUB, xb.shape[1])
                xb[pl.ds(r0, MOE_SUB), :] = xs.astype(_BF)
                yacc[pl.ds(r0, MOE_SUB), :] = bias_rows
                return c
            lax.fori_loop(0, ns, cast, 0)

        wgb[...] = wg_ref[0].astype(_BF)
        wub[...] = wu_ref[0].astype(_BF)
        wdb[...] = wd_ref[0].astype(_BF)

        def compute(r0):
            xs = xb[pl.ds(r0, MOE_SUB), :]
            gg = jnp.dot(xs, wgb[...], preferred_element_type=_F32) + bg_ref[0]
            uu = jnp.dot(xs, wub[...], preferred_element_type=_F32) + bu_ref[0]
            gg = jnp.minimum(gg, SWIGLU_LIMIT)
            uu = jnp.clip(uu, -SWIGLU_LIMIT, SWIGLU_LIMIT)
            act = (uu + 1.0) * (gg * jax.nn.sigmoid(SWIGLU_ALPHA * gg))
            yacc[pl.ds(r0, MOE_SUB), :] += jnp.dot(act.astype(_BF), wdb[...], preferred_element_type=_F32)

        def step(s0, n_sub):
            def body(prefetch):
                if prefetch:
                    t8_0 = issued[0] // 8
                    for i8 in range(MOE_SUB // 8):
                        issue8(next_base, t8_0 + i8)
                    issued[0] = issued[0] + MOE_SUB
                for k in range(n_sub):
                    compute(pl.multiple_of((s0 + k) * MOE_SUB, MOE_SUB))

            lax.cond(issued[0] < next_rows, lambda: body(True), lambda: body(False))

            @pl.when(j == MOE_J - 1)
            def _():
                for k in range(n_sub):
                    out_copy(base, pl.multiple_of((s0 + k) * MOE_SUB, MOE_SUB)).start()

        def pair(p, c):
            step(2 * p, 2)
            return c
        lax.fori_loop(0, ns // 2, pair, 0)

        @pl.when(ns % 2 == 1)
        def _():
            step(ns - 1, 1)

        @pl.when(j == MOE_J - 1)
        def _():
            def drain(s, c):
                out_copy(base, pl.multiple_of(s * MOE_SUB, MOE_SUB)).wait()
                return c
            lax.fori_loop(0, ns, drain, 0)

    @pl.when((g == pl.num_programs(0) - 1) & (j == MOE_J - 1))
    def _():
        tail = pl.multiple_of(tail_ref[0], MOE_SUB)
        n_tail = (y_hbm.shape[0] - tail) // MOE_SUB
        yacc[pl.ds(0, MOE_SUB), :] = jnp.zeros((MOE_SUB, yacc.shape[1]), _F32)

        def tail_copy(s):
            dst = pl.multiple_of(tail + s * MOE_SUB, MOE_SUB)
            return pltpu.make_async_copy(yacc.at[pl.ds(0, MOE_SUB)], y_hbm.at[pl.ds(dst, MOE_SUB)], sem_o)

        def start(s, c):
            tail_copy(s).start()
            return c
        lax.fori_loop(0, n_tail, start, 0)

        def done(s, c):
            tail_copy(s).wait()
            return c
        lax.fori_loop(0, n_tail, done, 0)


def _moe_experts(h, w_gu, b_gu, w_down, b_down, layer, slot_tok, ge, gs, nsub, tail):
    n, d = h.shape
    n_rows = w_gu.shape[0] * N_EXPERTS
    w_gu = w_gu.reshape(n_rows, d, 2 * EXPERT_FF)
    w_down = w_down.reshape(n_rows, EXPERT_FF, d)
    b_gu = b_gu.reshape(n_rows, 1, 2 * EXPERT_FF)
    b_down = b_down.reshape(n_rows, 1, d)
    ge = ge + layer * N_EXPERTS
    n_slots, n_groups = _moe_layout(n)
    f, jn = MOE_F, MOE_J

    def jj(g, j, nsub_ref):
        return jnp.where(nsub_ref[g] > 0, j, jn - 1)

    grid_spec = pltpu.PrefetchScalarGridSpec(
        num_scalar_prefetch=5,
        grid=(n_groups, jn),
        in_specs=[
            pl.BlockSpec(memory_space=pl.ANY),
            pl.BlockSpec((1, d, f), lambda g, j, ge, gs, ns, tl, tk: (ge[g], 0, jj(g, j, ns))),
            pl.BlockSpec((1, d, f), lambda g, j, ge, gs, ns, tl, tk: (ge[g], 0, jn + jj(g, j, ns))),
            pl.BlockSpec((1, f, d), lambda g, j, ge, gs, ns, tl, tk: (ge[g], jj(g, j, ns), 0)),
            pl.BlockSpec((1, 1, f), lambda g, j, ge, gs, ns, tl, tk: (ge[g], 0, jj(g, j, ns))),
            pl.BlockSpec((1, 1, f), lambda g, j, ge, gs, ns, tl, tk: (ge[g], 0, jn + jj(g, j, ns))),
            pl.BlockSpec((1, 1, d), lambda g, j, ge, gs, ns, tl, tk: (ge[g], 0, 0)),
        ],
        out_specs=pl.BlockSpec(memory_space=pl.ANY),
        scratch_shapes=[
            pltpu.VMEM((MOE_RMAX // 8, 8, d), _F32),
            pltpu.VMEM((MOE_RMAX, d), _BF),
            pltpu.VMEM((MOE_RMAX, d), _F32),
            pltpu.VMEM((d, f), _BF),
            pltpu.VMEM((d, f), _BF),
            pltpu.VMEM((f, d), _BF),
            pltpu.SMEM((1,), jnp.int32),
            pltpu.SemaphoreType.DMA(()),
            pltpu.SemaphoreType.DMA(()),
        ],
    )
    return pl.pallas_call(
        _moe_kernel,
        grid_spec=grid_spec,
        out_shape=jax.ShapeDtypeStruct((n_slots, d), _F32),
        compiler_params=_params(("arbitrary", "arbitrary")),
        name="moe_experts",
    )(ge, gs, nsub, tail, slot_tok, h, w_gu, w_gu, w_down, b_gu, b_gu, b_down)


def _combine_ln2_kernel(pos_ref, h_ref, hb_ref, p_ref, gate_ref, wg_ref, bg_ref, wp_ref, g_ref, b_ref, y_hbm,
                        x_ref, xb_ref, yg, sem, *, tm):
    i = pl.program_id(0)
    base = i * (tm * TOP_K)

    def issue(r8):
        for u in range(8):
            for k in range(TOP_K):
                s = pos_ref[base + (r8 * 8 + u) * TOP_K + k]
                pltpu.make_async_copy(y_hbm.at[pl.ds(s, 1)], yg.at[k, r8, pl.ds(u, 1)], sem).start()

    d = h_ref.shape[1]
    n_ch = C2_CHUNKS
    cw = d // n_ch
    hb = hb_ref[...]
    pb = p_ref[...].astype(_BF)
    parts = []
    for c in range(n_ch):
        for r8 in range(c * (tm // 8 // n_ch), (c + 1) * (tm // 8 // n_ch)):
            issue(r8)
        cs = slice(c * cw, (c + 1) * cw)
        gate_v = jax.nn.sigmoid(jnp.dot(hb, wg_ref[:, cs], preferred_element_type=_F32) + bg_ref[:, cs])
        pp = jnp.dot(pb, wp_ref[:, cs], preferred_element_type=_F32)
        parts.append(DEEPNORM_ALPHA * h_ref[:, cs] + gate_v * pp)
    acc = jnp.concatenate(parts, axis=1)

    def wait_row(r, c):
        pltpu.make_async_copy(y_hbm.at[pl.ds(0, 1)], yg.at[0, 0, pl.ds(0, 1)], sem).wait()
        return c
    lax.fori_loop(0, tm * TOP_K, wait_row, 0, unroll=8)

    gates = gate_ref[...]
    for k in range(TOP_K):
        acc = acc + gates[:, k:k + 1] * yg[k].reshape(acc.shape)
    xn = _ln(acc, g_ref[...], b_ref[...])
    x_ref[...] = xn
    xb_ref[...] = xn.astype(_BF)


def _combine_ln2(h, hb, p_i, gates, pos, y_sorted, w_g_b, b_g, w_p_b, ln_g, ln_b):
    n, d = h.shape
    tm = C2_TM
    row = lambda i, pos_ref: (i, 0)
    const = lambda i, pos_ref: (0, 0)
    grid_spec = pltpu.PrefetchScalarGridSpec(
        num_scalar_prefetch=1,
        grid=(n // tm,),
        in_specs=[
            pl.BlockSpec((tm, d), row), pl.BlockSpec((tm, d), row), pl.BlockSpec((tm, PLE_DIM), row),
            pl.BlockSpec((tm, TOP_K), row),
            pl.BlockSpec((d, d), const), pl.BlockSpec((1, d), const), pl.BlockSpec((PLE_DIM, d), const),
            pl.BlockSpec((1, d), const), pl.BlockSpec((1, d), const),
            pl.BlockSpec(memory_space=pl.ANY),
        ],
        out_specs=[pl.BlockSpec((tm, d), row), pl.BlockSpec((tm, d), row)],
        scratch_shapes=[pltpu.VMEM((TOP_K, tm // 8, 8, d), _F32), pltpu.SemaphoreType.DMA(())],
    )
    return pl.pallas_call(
        functools.partial(_combine_ln2_kernel, tm=tm),
        grid_spec=grid_spec,
        out_shape=[jax.ShapeDtypeStruct((n, d), _F32), jax.ShapeDtypeStruct((n, d), _BF)],
        compiler_params=_params(("arbitrary",)),
        name="combine_ln2",
    )(pos.reshape(-1), h, hb, p_i, gates, w_g_b, b_g, w_p_b, ln_g, ln_b, y_sorted)


def _mm_kernel(a_ref, w_ref, o_ref):
    o_ref[...] = jnp.dot(a_ref[...], w_ref[...], preferred_element_type=_F32)


def _mm(a_b, w_b):
    n, d = a_b.shape
    nout = w_b.shape[1]
    tm, tn = MM_TM, MM_TN
    return pl.pallas_call(
        _mm_kernel,
        grid=(n // tm, nout // tn),
        in_specs=[pl.BlockSpec((tm, d), lambda i, j: (i, 0)), pl.BlockSpec((d, tn), lambda i, j: (0, j))],
        out_specs=pl.BlockSpec((tm, tn), lambda i, j: (i, j)),
        out_shape=jax.ShapeDtypeStruct((n, nout), _F32),
        compiler_params=_params(("parallel", "arbitrary")),
        name="mm",
    )(a_b, w_b)


def _attn_kernel(slopes_ref, q0_ref, q1_ref, q2_ref, kc_ref, kp_ref, vc_ref, vp_ref, o_ref,
                 bias_s, m_s, l_s, acc_s):
    j = pl.program_id(1)
    h = pl.program_id(2)
    neg_slope = -slopes_ref[h]
    qi = lax.broadcasted_iota(jnp.int32, (QBLK, 2 * QBLK), 0)
    ki = lax.broadcasted_iota(jnp.int32, (QBLK, 2 * QBLK), 1)
    dist = qi + QBLK - ki
    n_groups = B_GROUPS
    for g, d in enumerate(B_DILATIONS):
        n_back = B_WINDOWS[g] // d
        valid = (dist >= 0) & (dist <= n_back)
        bias = jnp.where(valid, neg_slope * (dist * d).astype(_F32), NEG_INF)
        bias_s[g] = bias
        bias_s[n_groups + g] = jnp.where(j == 0, jnp.where(ki < QBLK, NEG_INF, bias), bias)

    scale = np.float32(HEAD_DIM ** -0.5)

    def block(g, d, q_ref, start, kband, vband, bias):
        q = q_ref[pl.ds(start, QBLK, stride=d), :].astype(_BF) if d > 1 else q_ref[pl.ds(start, QBLK), :].astype(_BF)
        s = lax.dot_general(q, kband, (((1,), (1,)), ((), ())), preferred_element_type=_F32)
        s = s * scale + bias
        m = jnp.max(s, axis=-1, keepdims=True)
        p = jnp.exp(s - m)
        l = jnp.sum(p, axis=-1, keepdims=True)
        pv = jnp.dot(p.astype(_BF), vband, preferred_element_type=_F32)
        mb = jnp.broadcast_to(m, (QBLK, HEAD_DIM))
        lb = jnp.broadcast_to(l, (QBLK, HEAD_DIM))
        if d > 1:
            idx = (g, pl.ds(start, QBLK, stride=d), slice(None))
        else:
            idx = (g, pl.ds(start, QBLK), slice(None))
        m_s[idx] = mb
        l_s[idx] = lb
        acc_s[idx] = pv

    def rows(ref, start, size, d):
        if d > 1:
            return ref[pl.ds(start, size, stride=d), :].astype(_BF)
        return ref[pl.ds(start, size), :].astype(_BF)

    for g, d in enumerate(B_DILATIONS):
        q_ref = (q0_ref, q1_ref, q2_ref)[g]
        span = QBLK * d
        nloc = ATT_T // span

        def first_block(r, g=g, d=d, q_ref=q_ref, span=span, nloc=nloc):
            pstart = (nloc - 1) * span + r
            kband = jnp.concatenate([rows(kp_ref, pstart, QBLK, d), rows(kc_ref, r, QBLK, d)], axis=0)
            vband = jnp.concatenate([rows(vp_ref, pstart, QBLK, d), rows(vc_ref, r, QBLK, d)], axis=0)
            block(g, d, q_ref, r, kband, vband, bias_s[n_groups + g])

        def later_block(r, n, g=g, d=d, q_ref=q_ref, span=span):
            bstart = (n - 1) * span + r
            kband = rows(kc_ref, bstart, 2 * QBLK, d)
            vband = rows(vc_ref, bstart, 2 * QBLK, d)
            block(g, d, q_ref, n * span + r, kband, vband, bias_s[g])

        if nloc == 1:
            def body(r, c, first_block=first_block):
                first_block(r)
                return c
            lax.fori_loop(0, d, body, 0, unroll=8)
        elif d == 1:
            first_block(0)

            def body(n, c, later_block=later_block):
                later_block(0, n)
                return c
            half = (nloc + 1) // 2
            lax.fori_loop(1, half + 1, body, 0, unroll=half)
            lax.fori_loop(half + 1, nloc, body, 0, unroll=nloc - half - 1)
        else:
            def body(r, c, first_block=first_block, later_block=later_block, nloc=nloc):
                first_block(r)
                for n in range(1, nloc):
                    later_block(r, n)
                return c
            lax.fori_loop(0, d, body, 0, unroll=2)

    m0, m1, m2 = m_s[0], m_s[1], m_s[2]
    mm = jnp.maximum(jnp.maximum(m0, m1), m2)
    w0 = jnp.exp(m0 - mm)
    w1 = jnp.exp(m1 - mm)
    w2 = jnp.exp(m2 - mm)
    num = w0 * acc_s[0] + w1 * acc_s[1] + w2 * acc_s[2]
    den = w0 * l_s[0] + w1 * l_s[1] + w2 * l_s[2]
    o_ref[...] = (num / den).astype(o_ref.dtype)


def _dilated_attention(q, kv, bsz, seq):
    n = q.shape[0]
    t = ATT_T
    tiles = seq // t
    hd = B_HEADS
    slopes = 2.0 ** (-8.0 * jnp.arange(1, hd + 1, dtype=_F32) / hd)

    def qspec(g):
        return pl.BlockSpec((t, HEAD_DIM), lambda b, j, h, s, g=g: (b * tiles + j, g * hd + h))

    def cur(off):
        return pl.BlockSpec((t, HEAD_DIM), lambda b, j, h, s: (b * tiles + j, off + h))

    def prev(off):
        return pl.BlockSpec((t, HEAD_DIM), lambda b, j, h, s: (b * tiles + jnp.maximum(j - 1, 0), off + h))

    grid_spec = pltpu.PrefetchScalarGridSpec(
        num_scalar_prefetch=1,
        grid=(bsz, tiles, hd),
        in_specs=[qspec(0), qspec(1), qspec(2), cur(0), prev(0), cur(hd), prev(hd)],
        out_specs=pl.BlockSpec((t, HEAD_DIM), lambda b, j, h, s: (b * tiles + j, h)),
        scratch_shapes=[
            pltpu.VMEM((2 * B_GROUPS, QBLK, 2 * QBLK), _F32),
            pltpu.VMEM((B_GROUPS, t, HEAD_DIM), _F32),
            pltpu.VMEM((B_GROUPS, t, HEAD_DIM), _F32),
            pltpu.VMEM((B_GROUPS, t, HEAD_DIM), _F32),
        ],
    )
    return pl.pallas_call(
        _attn_kernel,
        grid_spec=grid_spec,
        out_shape=jax.ShapeDtypeStruct((n, hd * HEAD_DIM), _BF),
        compiler_params=_params(("parallel", "parallel", "arbitrary")),
        name="dilated_attn",
    )(slopes, q, q, q, kv, kv, kv, kv)


def kernel(x, p, a_w_in, a_b_in, a_ln_g, a_ln_b, a_w_s, a_b_s, a_w_out, a_b_out, kv_w, b_w_q, b_w_o, ln1_g, ln1_b, ln2_g, ln2_b, router_w, router_b, moe_w_gu, moe_b_gu, moe_w_down, moe_b_down, ple_w_p, ple_w_g, ple_b_g):
    bsz, seq, d = x.shape
    n = bsz * seq
    xf = x.reshape(n, d)
    xb = xf.astype(_BF)
    row = lambda v: v.reshape(1, -1)
    kv = None
    for i in range(DEPTH):
        if i < N_A_LAYERS:
            gated = _gmlp_front(xb, a_w_in[i].astype(_BF), row(a_b_in[i]), row(a_ln_g[i]), row(a_ln_b[i]),
                                a_w_s[i], a_b_s[i].T)
            h, hb, idx, gates, rank, counts = _proj_ln(gated, a_w_out[i].astype(_BF), row(a_b_out[i]), xf,
                                         row(ln1_g[i]), row(ln1_b[i]), router_w[i], row(router_b[i]))
        else:
            jl = i - N_A_LAYERS
            q = _mm(xb, b_w_q[jl].astype(_BF))
            att = _dilated_attention(q, kv, bsz, seq)
            h, hb, idx, gates, rank, counts = _proj_ln(att, b_w_o[jl].astype(_BF), None, xf,
                                         row(ln1_g[i]), row(ln1_b[i]), router_w[i], row(router_b[i]))
        pos, slot_tok, ge, gs, nsub, tail = _routing(idx, rank, counts)
        y_sorted = _moe_experts(h, moe_w_gu, moe_b_gu, moe_w_down, moe_b_down, i,
                                slot_tok, ge, gs, nsub, tail)
        xf, xb = _combine_ln2(h, hb, p[i].reshape(n, PLE_DIM), gates, pos, y_sorted,
                              ple_w_g[i].astype(_BF), row(ple_b_g[i]), ple_w_p[i].astype(_BF),
                              row(ln2_g[i]), row(ln2_b[i]))
        if i == N_A_LAYERS - 1:
            kv = _mm(xb, kv_w.astype(_BF))
    return xf.reshape(bsz, seq, d)
```

```python
import functools

import jax
import jax.numpy as jnp
import numpy as np
from jax import lax
from jax.experimental import pallas as pl
from jax.experimental.pallas import tpu as pltpu

D_MODEL = 2048
DEPTH = 4
N_A_LAYERS = DEPTH // 2
CHUNK = 128
A_WIDTH = D_MODEL
A_GROUPS = 8
A_GROUP_CH = A_WIDTH // A_GROUPS
HEAD_DIM = 128
B_HEADS = D_MODEL // HEAD_DIM
B_WINDOWS = (128, 512, 2048)
B_DILATIONS = (1, 4, 16)
B_GROUPS = len(B_WINDOWS)
QBLK = 128
N_EXPERTS = 32
TOP_K = 4
EXPERT_FF = D_MODEL // 2
SWIGLU_LIMIT = 7.0
SWIGLU_ALPHA = 1.702
PLE_DIM = 256
LN_EPS = 1e-5
DEEPNORM_ALPHA = (2.0 * DEPTH) ** 0.25
NEG_INF = -1e30

V7X_VMEM_BYTES = 64 * 1024 * 1024
VMEM_LIMIT = 56 * 1024 * 1024
GF_TM, GF_TK = 512, 512
PL_TM = 512
PL_PARTS = 2
MM_TM, MM_TN = 1024, 1024
ATT_T = QBLK * max(B_DILATIONS)
MOE_F = 256
MOE_J = EXPERT_FF // MOE_F
MOE_SUB = 256
MOE_RMAX = 1280
C2_TM = 256
C2_CHUNKS = 4

_BF = jnp.bfloat16
_F32 = jnp.float32


def _params(sem):
    return pltpu.CompilerParams(dimension_semantics=sem, vmem_limit_bytes=VMEM_LIMIT)


def _ln(y, g, b):
    mu = jnp.mean(y, axis=-1, keepdims=True)
    yc = y - mu
    var = jnp.mean(yc * yc, axis=-1, keepdims=True)
    return yc * lax.rsqrt(var + LN_EPS) * g + b


def _gmlp_front_kernel(x_ref, w_ref, b_ref, lng_ref, lnb_ref, ws_ref, bst_ref, o_ref, acc_ref, *, nk, tm):
    k = pl.program_id(1)

    @pl.when(k == 0)
    def _():
        acc_ref[...] = jnp.dot(x_ref[...], w_ref[...], preferred_element_type=_F32)

    @pl.when(k > 0)
    def _():
        acc_ref[...] += jnp.dot(x_ref[...], w_ref[...], preferred_element_type=_F32)

    @pl.when(k == nk - 1)
    def _():
        row = lax.broadcasted_iota(jnp.int32, (CHUNK, CHUNK), 0)
        col = lax.broadcasted_iota(jnp.int32, (CHUNK, CHUNK), 1)
        causal = row >= col
        sqrt_half = np.float32(np.sqrt(0.5))

        def chunk(c, carry):
            r0 = pl.multiple_of(c * CHUNK, CHUNK)
            z = acc_ref[pl.ds(r0, CHUNK), :] + b_ref[...]
            z = 0.5 * z * (1.0 + lax.erf(z * sqrt_half))
            u = z[:, :A_WIDTH]
            v = _ln(z[:, A_WIDTH:], lng_ref[...], lnb_ref[...]).astype(_BF)
            for g in range(A_GROUPS):
                wg = jnp.where(causal, ws_ref[g], 0.0).astype(_BF)
                cs = slice(g * A_GROUP_CH, (g + 1) * A_GROUP_CH)
                vs = jnp.dot(wg, v[:, cs], preferred_element_type=_F32) + bst_ref[:, g:g + 1]
                o_ref[pl.ds(r0, CHUNK), cs] = (u[:, cs] * vs).astype(_BF)
            return carry

        lax.fori_loop(0, tm // CHUNK, chunk, 0)


def _gmlp_front(xb, w_in_b, b_in, ln_g, ln_b, w_s, b_s_t):
    n, d = xb.shape
    tm, tk = GF_TM, GF_TK
    nk = d // tk
    return pl.pallas_call(
        functools.partial(_gmlp_front_kernel, nk=nk, tm=tm),
        grid=(n // tm, nk),
        in_specs=[
            pl.BlockSpec((tm, tk), lambda i, k: (i, k)),
            pl.BlockSpec((tk, 2 * A_WIDTH), lambda i, k: (k, 0)),
            pl.BlockSpec((1, 2 * A_WIDTH), lambda i, k: (0, 0)),
            pl.BlockSpec((1, A_WIDTH), lambda i, k: (0, 0)),
            pl.BlockSpec((1, A_WIDTH), lambda i, k: (0, 0)),
            pl.BlockSpec((A_GROUPS, CHUNK, CHUNK), lambda i, k: (0, 0, 0)),
            pl.BlockSpec((CHUNK, A_GROUPS), lambda i, k: (0, 0)),
        ],
        out_specs=pl.BlockSpec((tm, A_WIDTH), lambda i, k: (i, 0)),
        out_shape=jax.ShapeDtypeStruct((n, A_WIDTH), _BF),
        scratch_shapes=[pltpu.VMEM((tm, 2 * A_WIDTH), _F32)],
        compiler_params=_params(("parallel", "arbitrary")),
        name="gmlp_front",
    )(xb, w_in_b, b_in, ln_g, ln_b, w_s, b_s_t)


def _proj_ln_kernel(*refs, has_bias):
    if has_bias:
        a_ref, w_ref, b_ref, xres_ref, g_ref, bb_ref, wr_ref, br_ref = refs[:8]
    else:
        a_ref, w_ref, xres_ref, g_ref, bb_ref, wr_ref, br_ref = refs[:7]
        b_ref = None
    h_ref, hb_ref, idx_ref, gate_ref, rank_ref, cnt_ref = refs[-6:]
    wr = wr_ref[...]
    w_hi = wr.astype(_BF)
    w_lo = (wr - w_hi.astype(_F32)).astype(_BF)
    w_cat = jnp.concatenate([w_hi, w_lo], axis=1)
    rows = a_ref.shape[0] // PL_PARTS

    @pl.when(pl.program_id(0) == 0)
    def _():
        cnt_ref[...] = jnp.zeros_like(cnt_ref)
    earlier = (lax.broadcasted_iota(jnp.int32, (rows, rows), 0)
               > lax.broadcasted_iota(jnp.int32, (rows, rows), 1)).astype(_BF)
    for part in range(PL_PARTS):
        rs = slice(part * rows, (part + 1) * rows)
        mix = jnp.dot(a_ref[rs, :], w_ref[...], preferred_element_type=_F32)
        if has_bias:
            mix = mix + b_ref[...]
        h = _ln(DEEPNORM_ALPHA * xres_ref[rs, :] + mix, g_ref[...], bb_ref[...])
        h_ref[rs, :] = h
        h_hi = h.astype(_BF)
        hb_ref[rs, :] = h_hi

        h_lo = (h - h_hi.astype(_F32)).astype(_BF)
        r = (jnp.dot(h_hi, w_cat, preferred_element_type=_F32)
             + jnp.dot(h_lo, w_cat, preferred_element_type=_F32))
        logits = r[:, :N_EXPERTS] + r[:, N_EXPERTS:] + br_ref[...]
        lane = lax.broadcasted_iota(jnp.int32, logits.shape, 1)
        cur = logits
        vals, idxs = [], []
        for _ in range(TOP_K):
            m = jnp.max(cur, axis=-1, keepdims=True)
            i = jnp.min(jnp.where(cur == m, lane, N_EXPERTS), axis=-1, keepdims=True)
            vals.append(m)
            idxs.append(i)
            cur = jnp.where(lane == i, -jnp.inf, cur)
        es = [jnp.exp(v - vals[0]) for v in vals]
        den = es[0] + es[1] + es[2] + es[3]
        colk = lax.broadcasted_iota(jnp.int32, (rows, TOP_K), 1)
        idx_out = jnp.zeros((rows, TOP_K), jnp.int32)
        gate_out = jnp.zeros((rows, TOP_K), _F32)
        for k in range(TOP_K):
            idx_out = jnp.where(colk == k, idxs[k], idx_out)
            gate_out = jnp.where(colk == k, es[k] / den, gate_out)
        idx_ref[rs, :] = idx_out
        gate_ref[rs, :] = gate_out

        hot = [lane == idxs[k] for k in range(TOP_K)]
        member = (hot[0] | hot[1] | hot[2] | hot[3]).astype(_F32)
        before = jnp.dot(earlier, member.astype(_BF), preferred_element_type=_F32) + cnt_ref[...]
        rank_out = jnp.zeros((rows, TOP_K), _F32)
        for k in range(TOP_K):
            rank_k = jnp.sum(jnp.where(hot[k], before, 0.0), axis=-1, keepdims=True)
            rank_out = jnp.where(colk == k, rank_k, rank_out)
        rank_ref[rs, :] = rank_out.astype(jnp.int32)
        cnt_ref[...] = cnt_ref[...] + jnp.sum(member, axis=0, keepdims=True)


def _proj_ln(a_b, w_b, bias, xres, ln_g, ln_b, w_r, b_r):
    n, d = xres.shape
    tm = PL_TM
    has_bias = bias is not None
    row = lambda i: (i, 0)
    const = lambda i: (0, 0)
    in_specs = [pl.BlockSpec((tm, a_b.shape[1]), row), pl.BlockSpec(w_b.shape, const)]
    args = [a_b, w_b]
    if has_bias:
        in_specs.append(pl.BlockSpec((1, d), const))
        args.append(bias)
    in_specs += [pl.BlockSpec((tm, d), row), pl.BlockSpec((1, d), const), pl.BlockSpec((1, d), const),
                 pl.BlockSpec((d, N_EXPERTS), const), pl.BlockSpec((1, N_EXPERTS), const)]
    args += [xres, ln_g, ln_b, w_r, b_r]
    return pl.pallas_call(
        functools.partial(_proj_ln_kernel, has_bias=has_bias),
        grid=(n // tm,),
        in_specs=in_specs,
        out_specs=[pl.BlockSpec((tm, d), row), pl.BlockSpec((tm, d), row),
                   pl.BlockSpec((tm, TOP_K), row), pl.BlockSpec((tm, TOP_K), row),
                   pl.BlockSpec((tm, TOP_K), row), pl.BlockSpec((1, N_EXPERTS), const)],
        out_shape=[jax.ShapeDtypeStruct((n, d), _F32), jax.ShapeDtypeStruct((n, d), _BF),
                   jax.ShapeDtypeStruct((n, TOP_K), jnp.int32), jax.ShapeDtypeStruct((n, TOP_K), _F32),
                   jax.ShapeDtypeStruct((n, TOP_K), jnp.int32), jax.ShapeDtypeStruct((1, N_EXPERTS), _F32)],
        compiler_params=_params(("arbitrary",)),
        name="proj_ln",
    )(*args)


def _moe_layout(n_tok):
    n_slots = n_tok * TOP_K + N_EXPERTS * MOE_SUB
    n_groups = N_EXPERTS + (n_tok * TOP_K) // MOE_RMAX
    return n_slots, n_groups


def _slot_tokens_kernel(pos_ref, tok_ref, zeros_ref, sem):
    zeros_ref[...] = jnp.zeros_like(zeros_ref)
    fill = pltpu.make_async_copy(zeros_ref, tok_ref, sem)
    fill.start()
    fill.wait()

    def place(a, c):
        tok_ref[pos_ref[a]] = lax.shift_right_logical(a, TOP_K.bit_length() - 1)
        return c
    lax.fori_loop(0, pos_ref.shape[0], place, 0, unroll=8)


def _slot_tokens(pos_flat, n_slots):
    assert TOP_K & (TOP_K - 1) == 0
    return pl.pallas_call(
        _slot_tokens_kernel,
        in_specs=[pl.BlockSpec(memory_space=pltpu.SMEM)],
        out_specs=pl.BlockSpec(memory_space=pltpu.SMEM),
        out_shape=jax.ShapeDtypeStruct((n_slots,), jnp.int32),
        scratch_shapes=[pltpu.VMEM((n_slots,), jnp.int32), pltpu.SemaphoreType.DMA(())],
        name="slot_tokens",
    )(pos_flat)


def _routing(idx, rank, counts):
    n = idx.shape[0]
    n_slots, n_groups = _moe_layout(n)
    counts = counts.reshape(N_EXPERTS).astype(jnp.int32)
    padded = (counts + MOE_SUB - 1) // MOE_SUB * MOE_SUB
    pstart = jnp.cumsum(padded) - padded
    lanes = 128
    pos = (pstart[idx.reshape(-1, lanes)] + rank.reshape(-1, lanes)).astype(jnp.int32).reshape(-1)
    slot_tok = _slot_tokens(pos, n_slots)
    ng = (counts + MOE_RMAX - 1) // MOE_RMAX
    gend = jnp.cumsum(ng)
    gstart = gend - ng
    total = gend[-1]
    gi = jnp.arange(n_groups, dtype=jnp.int32)
    gi_c = jnp.minimum(gi, total - 1)
    ge = jnp.searchsorted(gend, gi_c, side='right').astype(jnp.int32)
    local = gi_c - gstart[ge]
    gs = (pstart[ge] + local * MOE_RMAX).astype(jnp.int32)
    rows = jnp.clip(counts[ge] - local * MOE_RMAX, 0, MOE_RMAX)
    nsub = jnp.where(gi < total, (rows + MOE_SUB - 1) // MOE_SUB, 0).astype(jnp.int32)
    tail = jnp.sum(padded).astype(jnp.int32).reshape(1)
    return pos, slot_tok, ge, gs, nsub, tail


def _moe_kernel(ge_ref, gs_ref, nsub_ref, tail_ref, tok_ref,
                h_hbm, wg_ref, wu_ref, wd_ref, bg_ref, bu_ref, bd_ref,
                y_hbm,
                xg, xb, yacc, wgb, wub, wdb, issued, sem_g, sem_o):
    g = pl.program_id(0)
    j = pl.program_id(1)
    n_grp = pl.num_programs(0)
    ns = nsub_ref[g]
    g_next = jnp.minimum(g + 1, n_grp - 1)
    next_rows = jnp.where(g + 1 < n_grp, nsub_ref[g_next], 0) * MOE_SUB
    next_base = gs_ref[g_next]

    def gather_copy(t, i8, u):
        return pltpu.make_async_copy(h_hbm.at[pl.ds(t, 1)], xg.at[i8, pl.ds(u, 1)], sem_g)

    def issue8(slot_base, t8):
        for u in range(8):
            gather_copy(tok_ref[slot_base + t8 * 8 + u], t8, u).start()

    def out_copy(base, r0):
        dst = pl.multiple_of(base + r0, MOE_SUB)
        return pltpu.make_async_copy(yacc.at[pl.ds(r0, MOE_SUB)], y_hbm.at[pl.ds(dst, MOE_SUB)], sem_o)

    @pl.when((g == 0) & (j == 0))
    def _():
        issued[0] = 0

    @pl.when(ns > 0)
    def _():
        base = pl.multiple_of(gs_ref[g], MOE_SUB)

        @pl.when(j == 0)
        def _():
            def issue(t8, c):
                issue8(base, t8)
                return c
            lax.fori_loop(issued[0] // 8, ns * (MOE_SUB // 8), issue, 0)
            issued[0] = 0

            def wait_row(i, c):
                gather_copy(0, 0, 0).wait()
                return c

            def wait_sub(s, c):
                return lax.fori_loop(0, MOE_SUB, wait_row, c, unroll=8)
            lax.fori_loop(0, ns, wait_sub, 0)

            bias_rows = jnp.broadcast_to(bd_ref[0], (MOE_SUB, yacc.shape[1]))

            def cast(s, c):
                r0 = pl.multiple_of(s * MOE_SUB, MOE_SUB)
                t0 = s * (MOE_SUB // 8)
                xs = xg[pl.ds(t0, MOE_SUB // 8)].reshape(MOE_SUB, xb.shape[1])
                xb[pl.ds(r0, MOE_SUB), :] = xs.astype(_BF)
                yacc[pl.ds(r0, MOE_SUB), :] = bias_rows
                return c
            lax.fori_loop(0, ns, cast, 0)

        wgb[...] = wg_ref[0].astype(_BF)
        wub[...] = wu_ref[0].astype(_BF)
        wdb[...] = wd_ref[0].astype(_BF)

        def compute(r0):
            xs = xb[pl.ds(r0, MOE_SUB), :]
            gg = jnp.dot(xs, wgb[...], preferred_element_type=_F32) + bg_ref[0]
            uu = jnp.dot(xs, wub[...], preferred_element_type=_F32) + bu_ref[0]
            gg = jnp.minimum(gg, SWIGLU_LIMIT)
            uu = jnp.clip(uu, -SWIGLU_LIMIT, SWIGLU_LIMIT)
            act = (uu + 1.0) * (gg * jax.nn.sigmoid(SWIGLU_ALPHA * gg))
            yacc[pl.ds(r0, MOE_SUB), :] += jnp.dot(act.astype(_BF), wdb[...], preferred_element_type=_F32)

        def step(s0, n_sub):
            def body(prefetch):
                if prefetch:
                    t8_0 = issued[0] // 8
                    for i8 in range(MOE_SUB // 8):
                        issue8(next_base, t8_0 + i8)
                    issued[0] = issued[0] + MOE_SUB
                for k in range(n_sub):
                    compute(pl.multiple_of((s0 + k) * MOE_SUB, MOE_SUB))

            lax.cond(issued[0] < next_rows, lambda: body(True), lambda: body(False))

            @pl.when(j == MOE_J - 1)
            def _():
                for k in range(n_sub):
                    out_copy(base, pl.multiple_of((s0 + k) * MOE_SUB, MOE_SUB)).start()

        def pair(p, c):
            step(2 * p, 2)
            return c
        lax.fori_loop(0, ns // 2, pair, 0)

        @pl.when(ns % 2 == 1)
        def _():
            step(ns - 1, 1)

        @pl.when(j == MOE_J - 1)
        def _():
            def drain(s, c):
                out_copy(base, pl.multiple_of(s * MOE_SUB, MOE_SUB)).wait()
                return c
            lax.fori_loop(0, ns, drain, 0)

    @pl.when((g == pl.num_programs(0) - 1) & (j == MOE_J - 1))
    def _():
        tail = pl.multiple_of(tail_ref[0], MOE_SUB)
        n_tail = (y_hbm.shape[0] - tail) // MOE_SUB
        yacc[pl.ds(0, MOE_SUB), :] = jnp.zeros((MOE_SUB, yacc.shape[1]), _F32)

        def tail_copy(s):
            dst = pl.multiple_of(tail + s * MOE_SUB, MOE_SUB)
            return pltpu.make_async_copy(yacc.at[pl.ds(0, MOE_SUB)], y_hbm.at[pl.ds(dst, MOE_SUB)], sem_o)

        def start(s, c):
            tail_copy(s).start()
            return c
        lax.fori_loop(0, n_tail, start, 0)

        def done(s, c):
            tail_copy(s).wait()
            return c
        lax.fori_loop(0, n_tail, done, 0)


def _moe_experts(h, w_gu, b_gu, w_down, b_down, layer, slot_tok, ge, gs, nsub, tail):
    n, d = h.shape
    n_rows = w_gu.shape[0] * N_EXPERTS
    w_gu = w_gu.reshape(n_rows, d, 2 * EXPERT_FF)
    w_down = w_down.reshape(n_rows, EXPERT_FF, d)
    b_gu = b_gu.reshape(n_rows, 1, 2 * EXPERT_FF)
    b_down = b_down.reshape(n_rows, 1, d)
    ge = ge + layer * N_EXPERTS
    n_slots, n_groups = _moe_layout(n)
    f, jn = MOE_F, MOE_J

    def jj(g, j, nsub_ref):
        return jnp.where(nsub_ref[g] > 0, j, jn - 1)

    grid_spec = pltpu.PrefetchScalarGridSpec(
        num_scalar_prefetch=5,
        grid=(n_groups, jn),
        in_specs=[
            pl.BlockSpec(memory_space=pl.ANY),
            pl.BlockSpec((1, d, f), lambda g, j, ge, gs, ns, tl, tk: (ge[g], 0, jj(g, j, ns))),
            pl.BlockSpec((1, d, f), lambda g, j, ge, gs, ns, tl, tk: (ge[g], 0, jn + jj(g, j, ns))),
            pl.BlockSpec((1, f, d), lambda g, j, ge, gs, ns, tl, tk: (ge[g], jj(g, j, ns), 0)),
            pl.BlockSpec((1, 1, f), lambda g, j, ge, gs, ns, tl, tk: (ge[g], 0, jj(g, j, ns))),
            pl.BlockSpec((1, 1, f), lambda g, j, ge, gs, ns, tl, tk: (ge[g], 0, jn + jj(g, j, ns))),
            pl.BlockSpec((1, 1, d), lambda g, j, ge, gs, ns, tl, tk: (ge[g], 0, 0)),
        ],
        out_specs=pl.BlockSpec(memory_space=pl.ANY),
        scratch_shapes=[
            pltpu.VMEM((MOE_RMAX // 8, 8, d), _F32),
            pltpu.VMEM((MOE_RMAX, d), _BF),
            pltpu.VMEM((MOE_RMAX, d), _F32),
            pltpu.VMEM((d, f), _BF),
            pltpu.VMEM((d, f), _BF),
            pltpu.VMEM((f, d), _BF),
            pltpu.SMEM((1,), jnp.int32),
            pltpu.SemaphoreType.DMA(()),
            pltpu.SemaphoreType.DMA(()),
        ],
    )
    return pl.pallas_call(
        _moe_kernel,
        grid_spec=grid_spec,
        out_shape=jax.ShapeDtypeStruct((n_slots, d), _F32),
        compiler_params=_params(("arbitrary", "arbitrary")),
        name="moe_experts",
    )(ge, gs, nsub, tail, slot_tok, h, w_gu, w_gu, w_down, b_gu, b_gu, b_down)


def _combine_ln2_kernel(pos_ref, h_ref, hb_ref, p_ref, gate_ref, wg_ref, bg_ref, wp_ref, g_ref, b_ref, y_hbm,
                        x_ref, xb_ref, yg, sem, *, tm):
    i = pl.program_id(0)
    base = i * (tm * TOP_K)

    def issue(r8):
        for u in range(8):
            for k in range(TOP_K):
                s = pos_ref[base + (r8 * 8 + u) * TOP_K + k]
                pltpu.make_async_copy(y_hbm.at[pl.ds(s, 1)], yg.at[k, r8, pl.ds(u, 1)], sem).start()

    d = h_ref.shape[1]
    n_ch = C2_CHUNKS
    cw = d // n_ch
    hb = hb_ref[...]
    pb = p_ref[...].astype(_BF)
    parts = []
    for c in range(n_ch):
        for r8 in range(c * (tm // 8 // n_ch), (c + 1) * (tm // 8 // n_ch)):
            issue(r8)
        cs = slice(c * cw, (c + 1) * cw)
        gate_v = jax.nn.sigmoid(jnp.dot(hb, wg_ref[:, cs], preferred_element_type=_F32) + bg_ref[:, cs])
        pp = jnp.dot(pb, wp_ref[:, cs], preferred_element_type=_F32)
        parts.append(DEEPNORM_ALPHA * h_ref[:, cs] + gate_v * pp)
    acc = jnp.concatenate(parts, axis=1)

    def wait_row(r, c):
        pltpu.make_async_copy(y_hbm.at[pl.ds(0, 1)], yg.at[0, 0, pl.ds(0, 1)], sem).wait()
        return c
    lax.fori_loop(0, tm * TOP_K, wait_row, 0, unroll=8)

    gates = gate_ref[...]
    for k in range(TOP_K):
        acc = acc + gates[:, k:k + 1] * yg[k].reshape(acc.shape)
    xn = _ln(acc, g_ref[...], b_ref[...])
    x_ref[...] = xn
    xb_ref[...] = xn.astype(_BF)


def _combine_ln2(h, hb, p_i, gates, pos, y_sorted, w_g_b, b_g, w_p_b, ln_g, ln_b):
    n, d = h.shape
    tm = C2_TM
    row = lambda i, pos_ref: (i, 0)
    const = lambda i, pos_ref: (0, 0)
    grid_spec = pltpu.PrefetchScalarGridSpec(
        num_scalar_prefetch=1,
        grid=(n // tm,),
        in_specs=[
            pl.BlockSpec((tm, d), row), pl.BlockSpec((tm, d), row), pl.BlockSpec((tm, PLE_DIM), row),
            pl.BlockSpec((tm, TOP_K), row),
            pl.BlockSpec((d, d), const), pl.BlockSpec((1, d), const), pl.BlockSpec((PLE_DIM, d), const),
            pl.BlockSpec((1, d), const), pl.BlockSpec((1, d), const),
            pl.BlockSpec(memory_space=pl.ANY),
        ],
        out_specs=[pl.BlockSpec((tm, d), row), pl.BlockSpec((tm, d), row)],
        scratch_shapes=[pltpu.VMEM((TOP_K, tm // 8, 8, d), _F32), pltpu.SemaphoreType.DMA(())],
    )
    return pl.pallas_call(
        functools.partial(_combine_ln2_kernel, tm=tm),
        grid_spec=grid_spec,
        out_shape=[jax.ShapeDtypeStruct((n, d), _F32), jax.ShapeDtypeStruct((n, d), _BF)],
        compiler_params=_params(("arbitrary",)),
        name="combine_ln2",
    )(pos.reshape(-1), h, hb, p_i, gates, w_g_b, b_g, w_p_b, ln_g, ln_b, y_sorted)


def _mm_kernel(a_ref, w_ref, o_ref):
    o_ref[...] = jnp.dot(a_ref[...], w_ref[...], preferred_element_type=_F32)


def _mm(a_b, w_b):
    n, d = a_b.shape
    nout = w_b.shape[1]
    tm, tn = MM_TM, MM_TN
    return pl.pallas_call(
        _mm_kernel,
        grid=(n // tm, nout // tn),
        in_specs=[pl.BlockSpec((tm, d), lambda i, j: (i, 0)), pl.BlockSpec((d, tn), lambda i, j: (0, j))],
        out_specs=pl.BlockSpec((tm, tn), lambda i, j: (i, j)),
        out_shape=jax.ShapeDtypeStruct((n, nout), _F32),
        compiler_params=_params(("parallel", "arbitrary")),
        name="mm",
    )(a_b, w_b)


def _attn_kernel(slopes_ref, q0_ref, q1_ref, q2_ref, kc_ref, kp_ref, vc_ref, vp_ref, o_ref,
                 bias_s, m_s, l_s, acc_s):
    j = pl.program_id(1)
    h = pl.program_id(2)
    neg_slope = -slopes_ref[h]
    qi = lax.broadcasted_iota(jnp.int32, (QBLK, 2 * QBLK), 0)
    ki = lax.broadcasted_iota(jnp.int32, (QBLK, 2 * QBLK), 1)
    dist = qi + QBLK - ki
    n_groups = B_GROUPS
    for g, d in enumerate(B_DILATIONS):
        n_back = B_WINDOWS[g] // d
        valid = (dist >= 0) & (dist <= n_back)
        bias = jnp.where(valid, neg_slope * (dist * d).astype(_F32), NEG_INF)
        bias_s[g] = bias
        bias_s[n_groups + g] = jnp.where(j == 0, jnp.where(ki < QBLK, NEG_INF, bias), bias)

    scale = np.float32(HEAD_DIM ** -0.5)

    def block(g, d, q_ref, start, kband, vband, bias):
        q = q_ref[pl.ds(start, QBLK, stride=d), :].astype(_BF) if d > 1 else q_ref[pl.ds(start, QBLK), :].astype(_BF)
        s = lax.dot_general(q, kband, (((1,), (1,)), ((), ())), preferred_element_type=_F32)
        s = s * scale + bias
        m = jnp.max(s, axis=-1, keepdims=True)
        p = jnp.exp(s - m)
        l = jnp.sum(p, axis=-1, keepdims=True)
        pv = jnp.dot(p.astype(_BF), vband, preferred_element_type=_F32)
        mb = jnp.broadcast_to(m, (QBLK, HEAD_DIM))
        lb = jnp.broadcast_to(l, (QBLK, HEAD_DIM))
        if d > 1:
            idx = (g, pl.ds(start, QBLK, stride=d), slice(None))
        else:
            idx = (g, pl.ds(start, QBLK), slice(None))
        m_s[idx] = mb
        l_s[idx] = lb
        acc_s[idx] = pv

    def rows(ref, start, size, d):
        if d > 1:
            return ref[pl.ds(start, size, stride=d), :].astype(_BF)
        return ref[pl.ds(start, size), :].astype(_BF)

    for g, d in enumerate(B_DILATIONS):
        q_ref = (q0_ref, q1_ref, q2_ref)[g]
        span = QBLK * d
        nloc = ATT_T // span

        def first_block(r, g=g, d=d, q_ref=q_ref, span=span, nloc=nloc):
            pstart = (nloc - 1) * span + r
            kband = jnp.concatenate([rows(kp_ref, pstart, QBLK, d), rows(kc_ref, r, QBLK, d)], axis=0)
            vband = jnp.concatenate([rows(vp_ref, pstart, QBLK, d), rows(vc_ref, r, QBLK, d)], axis=0)
            block(g, d, q_ref, r, kband, vband, bias_s[n_groups + g])

        def later_block(r, n, g=g, d=d, q_ref=q_ref, span=span):
            bstart = (n - 1) * span + r
            kband = rows(kc_ref, bstart, 2 * QBLK, d)
            vband = rows(vc_ref, bstart, 2 * QBLK, d)
            block(g, d, q_ref, n * span + r, kband, vband, bias_s[g])

        if nloc == 1:
            def body(r, c, first_block=first_block):
                first_block(r)
                return c
            lax.fori_loop(0, d, body, 0, unroll=8)
        elif d == 1:
            first_block(0)

            def body(n, c, later_block=later_block):
                later_block(0, n)
                return c
            half = (nloc + 1) // 2
            lax.fori_loop(1, half + 1, body, 0, unroll=half)
            lax.fori_loop(half + 1, nloc, body, 0, unroll=nloc - half - 1)
        else:
            def body(r, c, first_block=first_block, later_block=later_block, nloc=nloc):
                first_block(r)
                for n in range(1, nloc):
                    later_block(r, n)
                return c
            lax.fori_loop(0, d, body, 0, unroll=d)

    m0, m1, m2 = m_s[0], m_s[1], m_s[2]
    mm = jnp.maximum(jnp.maximum(m0, m1), m2)
    w0 = jnp.exp(m0 - mm)
    w1 = jnp.exp(m1 - mm)
    w2 = jnp.exp(m2 - mm)
    num = w0 * acc_s[0] + w1 * acc_s[1] + w2 * acc_s[2]
    den = w0 * l_s[0] + w1 * l_s[1] + w2 * l_s[2]
    o_ref[...] = (num / den).astype(o_ref.dtype)


def _dilated_attention(q, kv, bsz, seq):
    n = q.shape[0]
    t = ATT_T
    tiles = seq // t
    hd = B_HEADS
    slopes = 2.0 ** (-8.0 * jnp.arange(1, hd + 1, dtype=_F32) / hd)

    def qspec(g):
        return pl.BlockSpec((t, HEAD_DIM), lambda b, j, h, s, g=g: (b * tiles + j, g * hd + h))

    def cur(off):
        return pl.BlockSpec((t, HEAD_DIM), lambda b, j, h, s: (b * tiles + j, off + h))

    def prev(off):
        return pl.BlockSpec((t, HEAD_DIM), lambda b, j, h, s: (b * tiles + jnp.maximum(j - 1, 0), off + h))

    grid_spec = pltpu.PrefetchScalarGridSpec(
        num_scalar_prefetch=1,
        grid=(bsz, tiles, hd),
        in_specs=[qspec(0), qspec(1), qspec(2), cur(0), prev(0), cur(hd), prev(hd)],
        out_specs=pl.BlockSpec((t, HEAD_DIM), lambda b, j, h, s: (b * tiles + j, h)),
        scratch_shapes=[
            pltpu.VMEM((2 * B_GROUPS, QBLK, 2 * QBLK), _F32),
            pltpu.VMEM((B_GROUPS, t, HEAD_DIM), _F32),
            pltpu.VMEM((B_GROUPS, t, HEAD_DIM), _F32),
            pltpu.VMEM((B_GROUPS, t, HEAD_DIM), _F32),
        ],
    )
    return pl.pallas_call(
        _attn_kernel,
        grid_spec=grid_spec,
        out_shape=jax.ShapeDtypeStruct((n, hd * HEAD_DIM), _BF),
        compiler_params=_params(("parallel", "parallel", "arbitrary")),
        name="dilated_attn",
    )(slopes, q, q, q, kv, kv, kv, kv)


def kernel(x, p, a_w_in, a_b_in, a_ln_g, a_ln_b, a_w_s, a_b_s, a_w_out, a_b_out, kv_w, b_w_q, b_w_o, ln1_g, ln1_b, ln2_g, ln2_b, router_w, router_b, moe_w_gu, moe_b_gu, moe_w_down, moe_b_down, ple_w_p, ple_w_g, ple_b_g):
    bsz, seq, d = x.shape
    n = bsz * seq
    xf = x.reshape(n, d)
    xb = xf.astype(_BF)
    row = lambda v: v.reshape(1, -1)
    kv = None
    for i in range(DEPTH):
        if i < N_A_LAYERS:
            gated = _gmlp_front(xb, a_w_in[i].astype(_BF), row(a_b_in[i]), row(a_ln_g[i]), row(a_ln_b[i]),
                                a_w_s[i], a_b_s[i].T)
            h, hb, idx, gates, rank, counts = _proj_ln(gated, a_w_out[i].astype(_BF), row(a_b_out[i]), xf,
                                         row(ln1_g[i]), row(ln1_b[i]), router_w[i], row(router_b[i]))
        else:
            jl = i - N_A_LAYERS
            q = _mm(xb, b_w_q[jl].astype(_BF))
            att = _dilated_attention(q, kv, bsz, seq)
            h, hb, idx, gates, rank, counts = _proj_ln(att, b_w_o[jl].astype(_BF), None, xf,
                                         row(ln1_g[i]), row(ln1_b[i]), router_w[i], row(router_b[i]))
        pos, slot_tok, ge, gs, nsub, tail = _routing(idx, rank, counts)
        y_sorted = _moe_experts(h, moe_w_gu, moe_b_gu, moe_w_down, moe_b_down, i,
                                slot_tok, ge, gs, nsub, tail)
        xf, xb = _combine_ln2(h, hb, p[i].reshape(n, PLE_DIM), gates, pos, y_sorted,
                              ple_w_g[i].astype(_BF), row(ple_b_g[i]), ple_w_p[i].astype(_BF),
                              row(ln2_g[i]), row(ln2_b[i]))
        if i == N_A_LAYERS - 1:
            kv = _mm(xb, kv_w.astype(_BF))
    return xf.reshape(bsz, seq, d)
```

```python
import functools

import jax
import jax.numpy as jnp
import numpy as np
from jax import lax
from jax.experimental import pallas as pl
from jax.experimental.pallas import tpu as pltpu

D_MODEL = 2048
DEPTH = 4
N_A_LAYERS = DEPTH // 2
CHUNK = 128
A_WIDTH = D_MODEL
A_GROUPS = 8
A_GROUP_CH = A_WIDTH // A_GROUPS
HEAD_DIM = 128
B_HEADS = D_MODEL // HEAD_DIM
B_WINDOWS = (128, 512, 2048)
B_DILATIONS = (1, 4, 16)
B_GROUPS = len(B_WINDOWS)
QBLK = 128
N_EXPERTS = 32
TOP_K = 4
EXPERT_FF = D_MODEL // 2
SWIGLU_LIMIT = 7.0
SWIGLU_ALPHA = 1.702
PLE_DIM = 256
LN_EPS = 1e-5
DEEPNORM_ALPHA = (2.0 * DEPTH) ** 0.25
NEG_INF = -1e30

V7X_VMEM_BYTES = 64 * 1024 * 1024
VMEM_LIMIT = 56 * 1024 * 1024
GF_TM, GF_TK = 512, 512
PL_TM = 512
PL_PARTS = 2
MM_TM, MM_TN = 1024, 1024
ATT_T = QBLK * max(B_DILATIONS)
MOE_F = 256
MOE_J = EXPERT_FF // MOE_F
MOE_SUB = 256
MOE_RMAX = 1280
C2_TM = 256
C2_CHUNKS = 4

_BF = jnp.bfloat16
_F32 = jnp.float32


def _params(sem):
    return pltpu.CompilerParams(dimension_semantics=sem, vmem_limit_bytes=VMEM_LIMIT)


def _ln(y, g, b):
    mu = jnp.mean(y, axis=-1, keepdims=True)
    yc = y - mu
    var = jnp.mean(yc * yc, axis=-1, keepdims=True)
    return yc * lax.rsqrt(var + LN_EPS) * g + b


def _gmlp_front_kernel(x_ref, w_ref, b_ref, lng_ref, lnb_ref, ws_ref, bst_ref, o_ref, acc_ref, *, nk, tm):
    k = pl.program_id(1)

    @pl.when(k == 0)
    def _():
        acc_ref[...] = jnp.dot(x_ref[...], w_ref[...], preferred_element_type=_F32)

    @pl.when(k > 0)
    def _():
        acc_ref[...] += jnp.dot(x_ref[...], w_ref[...], preferred_element_type=_F32)

    @pl.when(k == nk - 1)
    def _():
        row = lax.broadcasted_iota(jnp.int32, (CHUNK, CHUNK), 0)
        col = lax.broadcasted_iota(jnp.int32, (CHUNK, CHUNK), 1)
        causal = row >= col
        sqrt_half = np.float32(np.sqrt(0.5))

        def chunk(c, carry):
            r0 = pl.multiple_of(c * CHUNK, CHUNK)
            z = acc_ref[pl.ds(r0, CHUNK), :] + b_ref[...]
            z = 0.5 * z * (1.0 + lax.erf(z * sqrt_half))
            u = z[:, :A_WIDTH]
            v = _ln(z[:, A_WIDTH:], lng_ref[...], lnb_ref[...]).astype(_BF)
            for g in range(A_GROUPS):
                wg = jnp.where(causal, ws_ref[g], 0.0).astype(_BF)
                cs = slice(g * A_GROUP_CH, (g + 1) * A_GROUP_CH)
                vs = jnp.dot(wg, v[:, cs], preferred_element_type=_F32) + bst_ref[:, g:g + 1]
                o_ref[pl.ds(r0, CHUNK), cs] = (u[:, cs] * vs).astype(_BF)
            return carry

        lax.fori_loop(0, tm // CHUNK, chunk, 0)


def _gmlp_front(xb, w_in_b, b_in, ln_g, ln_b, w_s, b_s_t):
    n, d = xb.shape
    tm, tk = GF_TM, GF_TK
    nk = d // tk
    return pl.pallas_call(
        functools.partial(_gmlp_front_kernel, nk=nk, tm=tm),
        grid=(n // tm, nk),
        in_specs=[
            pl.BlockSpec((tm, tk), lambda i, k: (i, k)),
            pl.BlockSpec((tk, 2 * A_WIDTH), lambda i, k: (k, 0)),
            pl.BlockSpec((1, 2 * A_WIDTH), lambda i, k: (0, 0)),
            pl.BlockSpec((1, A_WIDTH), lambda i, k: (0, 0)),
            pl.BlockSpec((1, A_WIDTH), lambda i, k: (0, 0)),
            pl.BlockSpec((A_GROUPS, CHUNK, CHUNK), lambda i, k: (0, 0, 0)),
            pl.BlockSpec((CHUNK, A_GROUPS), lambda i, k: (0, 0)),
        ],
        out_specs=pl.BlockSpec((tm, A_WIDTH), lambda i, k: (i, 0)),
        out_shape=jax.ShapeDtypeStruct((n, A_WIDTH), _BF),
        scratch_shapes=[pltpu.VMEM((tm, 2 * A_WIDTH), _F32)],
        compiler_params=_params(("parallel", "arbitrary")),
        name="gmlp_front",
    )(xb, w_in_b, b_in, ln_g, ln_b, w_s, b_s_t)


def _proj_ln_kernel(*refs, has_bias):
    if has_bias:
        a_ref, w_ref, b_ref, xres_ref, g_ref, bb_ref, wr_ref, br_ref = refs[:8]
    else:
        a_ref, w_ref, xres_ref, g_ref, bb_ref, wr_ref, br_ref = refs[:7]
        b_ref = None
    h_ref, hb_ref, idx_ref, gate_ref, rank_ref, cnt_ref = refs[-6:]
    wr = wr_ref[...]
    w_hi = wr.astype(_BF)
    w_lo = (wr - w_hi.astype(_F32)).astype(_BF)
    w_cat = jnp.concatenate([w_hi, w_lo], axis=1)
    rows = a_ref.shape[0] // PL_PARTS

    @pl.when(pl.program_id(0) == 0)
    def _():
        cnt_ref[...] = jnp.zeros_like(cnt_ref)
    earlier = (lax.broadcasted_iota(jnp.int32, (rows, rows), 0)
               > lax.broadcasted_iota(jnp.int32, (rows, rows), 1)).astype(_BF)
    for part in range(PL_PARTS):
        rs = slice(part * rows, (part + 1) * rows)
        mix = jnp.dot(a_ref[rs, :], w_ref[...], preferred_element_type=_F32)
        if has_bias:
            mix = mix + b_ref[...]
        h = _ln(DEEPNORM_ALPHA * xres_ref[rs, :] + mix, g_ref[...], bb_ref[...])
        h_ref[rs, :] = h
        h_hi = h.astype(_BF)
        hb_ref[rs, :] = h_hi

        h_lo = (h - h_hi.astype(_F32)).astype(_BF)
        r = (jnp.dot(h_hi, w_cat, preferred_element_type=_F32)
             + jnp.dot(h_lo, w_cat, preferred_element_type=_F32))
        logits = r[:, :N_EXPERTS] + r[:, N_EXPERTS:] + br_ref[...]
        lane = lax.broadcasted_iota(jnp.int32, logits.shape, 1)
        cur = logits
        vals, idxs = [], []
        for _ in range(TOP_K):
            m = jnp.max(cur, axis=-1, keepdims=True)
            i = jnp.min(jnp.where(cur == m, lane, N_EXPERTS), axis=-1, keepdims=True)
            vals.append(m)
            idxs.append(i)
            cur = jnp.where(lane == i, -jnp.inf, cur)
        es = [jnp.exp(v - vals[0]) for v in vals]
        den = es[0] + es[1] + es[2] + es[3]
        colk = lax.broadcasted_iota(jnp.int32, (rows, TOP_K), 1)
        idx_out = jnp.zeros((rows, TOP_K), jnp.int32)
        gate_out = jnp.zeros((rows, TOP_K), _F32)
        for k in range(TOP_K):
            idx_out = jnp.where(colk == k, idxs[k], idx_out)
            gate_out = jnp.where(colk == k, es[k] / den, gate_out)
        idx_ref[rs, :] = idx_out
        gate_ref[rs, :] = gate_out

        hot = [lane == idxs[k] for k in range(TOP_K)]
        member = (hot[0] | hot[1] | hot[2] | hot[3]).astype(_F32)
        before = jnp.dot(earlier, member.astype(_BF), preferred_element_type=_F32) + cnt_ref[...]
        rank_out = jnp.zeros((rows, TOP_K), _F32)
        for k in range(TOP_K):
            rank_k = jnp.sum(jnp.where(hot[k], before, 0.0), axis=-1, keepdims=True)
            rank_out = jnp.where(colk == k, rank_k, rank_out)
        rank_ref[rs, :] = rank_out.astype(jnp.int32)
        cnt_ref[...] = cnt_ref[...] + jnp.sum(member, axis=0, keepdims=True)


def _proj_ln(a_b, w_b, bias, xres, ln_g, ln_b, w_r, b_r):
    n, d = xres.shape
    tm = PL_TM
    has_bias = bias is not None
    row = lambda i: (i, 0)
    const = lambda i: (0, 0)
    in_specs = [pl.BlockSpec((tm, a_b.shape[1]), row), pl.BlockSpec(w_b.shape, const)]
    args = [a_b, w_b]
    if has_bias:
        in_specs.append(pl.BlockSpec((1, d), const))
        args.append(bias)
    in_specs += [pl.BlockSpec((tm, d), row), pl.BlockSpec((1, d), const), pl.BlockSpec((1, d), const),
                 pl.BlockSpec((d, N_EXPERTS), const), pl.BlockSpec((1, N_EXPERTS), const)]
    args += [xres, ln_g, ln_b, w_r, b_r]
    return pl.pallas_call(
        functools.partial(_proj_ln_kernel, has_bias=has_bias),
        grid=(n // tm,),
        in_specs=in_specs,
        out_specs=[pl.BlockSpec((tm, d), row), pl.BlockSpec((tm, d), row),
                   pl.BlockSpec((tm, TOP_K), row), pl.BlockSpec((tm, TOP_K), row),
                   pl.BlockSpec((tm, TOP_K), row), pl.BlockSpec((1, N_EXPERTS), const)],
        out_shape=[jax.ShapeDtypeStruct((n, d), _F32), jax.ShapeDtypeStruct((n, d), _BF),
                   jax.ShapeDtypeStruct((n, TOP_K), jnp.int32), jax.ShapeDtypeStruct((n, TOP_K), _F32),
                   jax.ShapeDtypeStruct((n, TOP_K), jnp.int32), jax.ShapeDtypeStruct((1, N_EXPERTS), _F32)],
        compiler_params=_params(("arbitrary",)),
        name="proj_ln",
    )(*args)


def _moe_layout(n_tok):
    n_slots = n_tok * TOP_K + N_EXPERTS * MOE_SUB
    n_groups = N_EXPERTS + (n_tok * TOP_K) // MOE_RMAX
    return n_slots, n_groups


def _slot_tokens_kernel(pos_ref, tok_ref, zeros_ref, sem):
    zeros_ref[...] = jnp.zeros_like(zeros_ref)
    fill = pltpu.make_async_copy(zeros_ref, tok_ref, sem)
    fill.start()
    fill.wait()

    def place(a, c):
        tok_ref[pos_ref[a]] = lax.shift_right_logical(a, TOP_K.bit_length() - 1)
        return c
    lax.fori_loop(0, pos_ref.shape[0], place, 0, unroll=8)


def _slot_tokens(pos_flat, n_slots):
    assert TOP_K & (TOP_K - 1) == 0
    return pl.pallas_call(
        _slot_tokens_kernel,
        in_specs=[pl.BlockSpec(memory_space=pltpu.SMEM)],
        out_specs=pl.BlockSpec(memory_space=pltpu.SMEM),
        out_shape=jax.ShapeDtypeStruct((n_slots,), jnp.int32),
        scratch_shapes=[pltpu.VMEM((n_slots,), jnp.int32), pltpu.SemaphoreType.DMA(())],
        name="slot_tokens",
    )(pos_flat)


def _routing(idx, rank, counts):
    n = idx.shape[0]
    n_slots, n_groups = _moe_layout(n)
    counts = counts.reshape(N_EXPERTS).astype(jnp.int32)
    padded = (counts + MOE_SUB - 1) // MOE_SUB * MOE_SUB
    pstart = jnp.cumsum(padded) - padded
    lanes = 128
    pos = (pstart[idx.reshape(-1, lanes)] + rank.reshape(-1, lanes)).astype(jnp.int32).reshape(-1)
    slot_tok = _slot_tokens(pos, n_slots)
    ng = (counts + MOE_RMAX - 1) // MOE_RMAX
    gend = jnp.cumsum(ng)
    gstart = gend - ng
    total = gend[-1]
    gi = jnp.arange(n_groups, dtype=jnp.int32)
    gi_c = jnp.minimum(gi, total - 1)
    ge = jnp.searchsorted(gend, gi_c, side='right').astype(jnp.int32)
    local = gi_c - gstart[ge]
    gs = (pstart[ge] + local * MOE_RMAX).astype(jnp.int32)
    rows = jnp.clip(counts[ge] - local * MOE_RMAX, 0, MOE_RMAX)
    nsub = jnp.where(gi < total, (rows + MOE_SUB - 1) // MOE_SUB, 0).astype(jnp.int32)
    tail = jnp.sum(padded).astype(jnp.int32).reshape(1)
    return pos, slot_tok, ge, gs, nsub, tail


def _moe_kernel(ge_ref, gs_ref, nsub_ref, tail_ref, tok_ref,
                h_hbm, wg_ref, wu_ref, wd_ref, bg_ref, bu_ref, bd_ref,
                y_hbm,
                xg, xb, yacc, wgb, wub, wdb, issued, sem_g, sem_o):
    g = pl.program_id(0)
    j = pl.program_id(1)
    n_grp = pl.num_programs(0)
    ns = nsub_ref[g]
    g_next = jnp.minimum(g + 1, n_grp - 1)
    next_rows = jnp.where(g + 1 < n_grp, nsub_ref[g_next], 0) * MOE_SUB
    next_base = gs_ref[g_next]

    def gather_copy(t, i8, u):
        return pltpu.make_async_copy(h_hbm.at[pl.ds(t, 1)], xg.at[i8, pl.ds(u, 1)], sem_g)

    def issue8(slot_base, t8):
        for u in range(8):
            gather_copy(tok_ref[slot_base + t8 * 8 + u], t8, u).start()

    def out_copy(base, r0):
        dst = pl.multiple_of(base + r0, MOE_SUB)
        return pltpu.make_async_copy(yacc.at[pl.ds(r0, MOE_SUB)], y_hbm.at[pl.ds(dst, MOE_SUB)], sem_o)

    @pl.when((g == 0) & (j == 0))
    def _():
        issued[0] = 0

    @pl.when(ns > 0)
    def _():
        base = pl.multiple_of(gs_ref[g], MOE_SUB)

        @pl.when(j == 0)
        def _():
            def issue(t8, c):
                issue8(base, t8)
                return c
            lax.fori_loop(issued[0] // 8, ns * (MOE_SUB // 8), issue, 0)
            issued[0] = 0

            def wait_row(i, c):
                gather_copy(0, 0, 0).wait()
                return c

            def wait_sub(s, c):
                return lax.fori_loop(0, MOE_SUB, wait_row, c, unroll=8)
            lax.fori_loop(0, ns, wait_sub, 0)

            bias_rows = jnp.broadcast_to(bd_ref[0], (MOE_SUB, yacc.shape[1]))

            def cast(s, c):
                r0 = pl.multiple_of(s * MOE_SUB, MOE_SUB)
                t0 = s * (MOE_SUB // 8)
                xs = xg[pl.ds(t0, MOE_SUB // 8)].reshape(MOE_SUB, xb.shape[1])
                xb[pl.ds(r0, MOE_SUB), :] = xs.astype(_BF)
                yacc[pl.ds(r0, MOE_SUB), :] = bias_rows
                return c
            lax.fori_loop(0, ns, cast, 0)

        wgb[...] = wg_ref[0].astype(_BF)
        wub[...] = wu_ref[0].astype(_BF)
        wdb[...] = wd_ref[0].astype(_BF)

        def compute(r0):
            xs = xb[pl.ds(r0, MOE_SUB), :]
            gg = jnp.dot(xs, wgb[...], preferred_element_type=_F32) + bg_ref[0]
            uu = jnp.dot(xs, wub[...], preferred_element_type=_F32) + bu_ref[0]
            gg = jnp.minimum(gg, SWIGLU_LIMIT)
            uu = jnp.clip(uu, -SWIGLU_LIMIT, SWIGLU_LIMIT)
            act = (uu + 1.0) * (gg * jax.nn.sigmoid(SWIGLU_ALPHA * gg))
            yacc[pl.ds(r0, MOE_SUB), :] += jnp.dot(act.astype(_BF), wdb[...], preferred_element_type=_F32)

        def step(s0, n_sub):
            def body(prefetch):
                if prefetch:
                    t8_0 = issued[0] // 8
                    for i8 in range(MOE_SUB // 8):
                        issue8(next_base, t8_0 + i8)
                    issued[0] = issued[0] + MOE_SUB
                for k in range(n_sub):
                    compute(pl.multiple_of((s0 + k) * MOE_SUB, MOE_SUB))

            lax.cond(issued[0] < next_rows, lambda: body(True), lambda: body(False))

            @pl.when(j == MOE_J - 1)
            def _():
                for k in range(n_sub):
                    out_copy(base, pl.multiple_of((s0 + k) * MOE_SUB, MOE_SUB)).start()

        def pair(p, c):
            step(2 * p, 2)
            return c
        lax.fori_loop(0, ns // 2, pair, 0)

        @pl.when(ns % 2 == 1)
        def _():
            step(ns - 1, 1)

        @pl.when(j == MOE_J - 1)
        def _():
            def drain(s, c):
                out_copy(base, pl.multiple_of(s * MOE_SUB, MOE_SUB)).wait()
                return c
            lax.fori_loop(0, ns, drain, 0)

    @pl.when((g == pl.num_programs(0) - 1) & (j == MOE_J - 1))
    def _():
        tail = pl.multiple_of(tail_ref[0], MOE_SUB)
        n_tail = (y_hbm.shape[0] - tail) // MOE_SUB
        yacc[pl.ds(0, MOE_SUB), :] = jnp.zeros((MOE_SUB, yacc.shape[1]), _F32)

        def tail_copy(s):
            dst = pl.multiple_of(tail + s * MOE_SUB, MOE_SUB)
            return pltpu.make_async_copy(yacc.at[pl.ds(0, MOE_SUB)], y_hbm.at[pl.ds(dst, MOE_SUB)], sem_o)

        def start(s, c):
            tail_copy(s).start()
            return c
        lax.fori_loop(0, n_tail, start, 0)

        def done(s, c):
            tail_copy(s).wait()
            return c
        lax.fori_loop(0, n_tail, done, 0)


def _moe_experts(h, w_gu, b_gu, w_down, b_down, layer, slot_tok, ge, gs, nsub, tail):
    n, d = h.shape
    n_rows = w_gu.shape[0] * N_EXPERTS
    w_gu = w_gu.reshape(n_rows, d, 2 * EXPERT_FF)
    w_down = w_down.reshape(n_rows, EXPERT_FF, d)
    b_gu = b_gu.reshape(n_rows, 1, 2 * EXPERT_FF)
    b_down = b_down.reshape(n_rows, 1, d)
    ge = ge + layer * N_EXPERTS
    n_slots, n_groups = _moe_layout(n)
    f, jn = MOE_F, MOE_J

    def jj(g, j, nsub_ref):
        return jnp.where(nsub_ref[g] > 0, j, jn - 1)

    grid_spec = pltpu.PrefetchScalarGridSpec(
        num_scalar_prefetch=5,
        grid=(n_groups, jn),
        in_specs=[
            pl.BlockSpec(memory_space=pl.ANY),
            pl.BlockSpec((1, d, f), lambda g, j, ge, gs, ns, tl, tk: (ge[g], 0, jj(g, j, ns))),
            pl.BlockSpec((1, d, f), lambda g, j, ge, gs, ns, tl, tk: (ge[g], 0, jn + jj(g, j, ns))),
            pl.BlockSpec((1, f, d), lambda g, j, ge, gs, ns, tl, tk: (ge[g], jj(g, j, ns), 0)),
            pl.BlockSpec((1, 1, f), lambda g, j, ge, gs, ns, tl, tk: (ge[g], 0, jj(g, j, ns))),
            pl.BlockSpec((1, 1, f), lambda g, j, ge, gs, ns, tl, tk: (ge[g], 0, jn + jj(g, j, ns))),
            pl.BlockSpec((1, 1, d), lambda g, j, ge, gs, ns, tl, tk: (ge[g], 0, 0)),
        ],
        out_specs=pl.BlockSpec(memory_space=pl.ANY),
        scratch_shapes=[
            pltpu.VMEM((MOE_RMAX // 8, 8, d), _F32),
            pltpu.VMEM((MOE_RMAX, d), _BF),
            pltpu.VMEM((MOE_RMAX, d), _F32),
            pltpu.VMEM((d, f), _BF),
            pltpu.VMEM((d, f), _BF),
            pltpu.VMEM((f, d), _BF),
            pltpu.SMEM((1,), jnp.int32),
            pltpu.SemaphoreType.DMA(()),
            pltpu.SemaphoreType.DMA(()),
        ],
    )
    return pl.pallas_call(
        _moe_kernel,
        grid_spec=grid_spec,
        out_shape=jax.ShapeDtypeStruct((n_slots, d), _F32),
        compiler_params=_params(("arbitrary", "arbitrary")),
        name="moe_experts",
    )(ge, gs, nsub, tail, slot_tok, h, w_gu, w_gu, w_down, b_gu, b_gu, b_down)


def _combine_ln2_kernel(pos_ref, h_ref, hb_ref, p_ref, gate_ref, wg_ref, bg_ref, wp_ref, g_ref, b_ref, y_hbm,
                        x_ref, xb_ref, yg, sem, *, tm):
    i = pl.program_id(0)
    base = i * (tm * TOP_K)

    def issue(r8):
        for u in range(8):
            for k in range(TOP_K):
                s = pos_ref[base + (r8 * 8 + u) * TOP_K + k]
                pltpu.make_async_copy(y_hbm.at[pl.ds(s, 1)], yg.at[k, r8, pl.ds(u, 1)], sem).start(
                    priority=k % 2)

    d = h_ref.shape[1]
    n_ch = C2_CHUNKS
    cw = d // n_ch
    hb = hb_ref[...]
    pb = p_ref[...].astype(_BF)
    parts = []
    for c in range(n_ch):
        for r8 in range(c * (tm // 8 // n_ch), (c + 1) * (tm // 8 // n_ch)):
            issue(r8)
        cs = slice(c * cw, (c + 1) * cw)
        gate_v = jax.nn.sigmoid(jnp.dot(hb, wg_ref[:, cs], preferred_element_type=_F32) + bg_ref[:, cs])
        pp = jnp.dot(pb, wp_ref[:, cs], preferred_element_type=_F32)
        parts.append(DEEPNORM_ALPHA * h_ref[:, cs] + gate_v * pp)
    acc = jnp.concatenate(parts, axis=1)

    def wait_row(r, c):
        pltpu.make_async_copy(y_hbm.at[pl.ds(0, 1)], yg.at[0, 0, pl.ds(0, 1)], sem).wait()
        return c
    lax.fori_loop(0, tm * TOP_K, wait_row, 0, unroll=8)

    gates = gate_ref[...]
    for k in range(TOP_K):
        acc = acc + gates[:, k:k + 1] * yg[k].reshape(acc.shape)
    xn = _ln(acc, g_ref[...], b_ref[...])
    x_ref[...] = xn
    xb_ref[...] = xn.astype(_BF)


def _combine_ln2(h, hb, p_i, gates, pos, y_sorted, w_g_b, b_g, w_p_b, ln_g, ln_b):
    n, d = h.shape
    tm = C2_TM
    row = lambda i, pos_ref: (i, 0)
    const = lambda i, pos_ref: (0, 0)
    grid_spec = pltpu.PrefetchScalarGridSpec(
        num_scalar_prefetch=1,
        grid=(n // tm,),
        in_specs=[
            pl.BlockSpec((tm, d), row), pl.BlockSpec((tm, d), row), pl.BlockSpec((tm, PLE_DIM), row),
            pl.BlockSpec((tm, TOP_K), row),
            pl.BlockSpec((d, d), const), pl.BlockSpec((1, d), const), pl.BlockSpec((PLE_DIM, d), const),
            pl.BlockSpec((1, d), const), pl.BlockSpec((1, d), const),
            pl.BlockSpec(memory_space=pl.ANY),
        ],
        out_specs=[pl.BlockSpec((tm, d), row), pl.BlockSpec((tm, d), row)],
        scratch_shapes=[pltpu.VMEM((TOP_K, tm // 8, 8, d), _F32), pltpu.SemaphoreType.DMA(())],
    )
    return pl.pallas_call(
        functools.partial(_combine_ln2_kernel, tm=tm),
        grid_spec=grid_spec,
        out_shape=[jax.ShapeDtypeStruct((n, d), _F32), jax.ShapeDtypeStruct((n, d), _BF)],
        compiler_params=_params(("arbitrary",)),
        name="combine_ln2",
    )(pos.reshape(-1), h, hb, p_i, gates, w_g_b, b_g, w_p_b, ln_g, ln_b, y_sorted)


def _mm_kernel(a_ref, w_ref, o_ref):
    o_ref[...] = jnp.dot(a_ref[...], w_ref[...], preferred_element_type=_F32)


def _mm(a_b, w_b):
    n, d = a_b.shape
    nout = w_b.shape[1]
    tm, tn = MM_TM, MM_TN
    return pl.pallas_call(
        _mm_kernel,
        grid=(n // tm, nout // tn),
        in_specs=[pl.BlockSpec((tm, d), lambda i, j: (i, 0)), pl.BlockSpec((d, tn), lambda i, j: (0, j))],
        out_specs=pl.BlockSpec((tm, tn), lambda i, j: (i, j)),
        out_shape=jax.ShapeDtypeStruct((n, nout), _F32),
        compiler_params=_params(("parallel", "arbitrary")),
        name="mm",
    )(a_b, w_b)


def _attn_kernel(slopes_ref, q0_ref, q1_ref, q2_ref, kc_ref, kp_ref, vc_ref, vp_ref, o_ref,
                 bias_s, m_s, l_s, acc_s):
    j = pl.program_id(1)
    h = pl.program_id(2)
    neg_slope = -slopes_ref[h]
    qi = lax.broadcasted_iota(jnp.int32, (QBLK, 2 * QBLK), 0)
    ki = lax.broadcasted_iota(jnp.int32, (QBLK, 2 * QBLK), 1)
    dist = qi + QBLK - ki
    n_groups = B_GROUPS
    for g, d in enumerate(B_DILATIONS):
        n_back = B_WINDOWS[g] // d
        valid = (dist >= 0) & (dist <= n_back)
        bias = jnp.where(valid, neg_slope * (dist * d).astype(_F32), NEG_INF)
        bias_s[g] = bias
        bias_s[n_groups + g] = jnp.where(j == 0, jnp.where(ki < QBLK, NEG_INF, bias), bias)

    scale = np.float32(HEAD_DIM ** -0.5)

    def block(g, d, q_ref, start, kband, vband, bias):
        q = q_ref[pl.ds(start, QBLK, stride=d), :].astype(_BF) if d > 1 else q_ref[pl.ds(start, QBLK), :].astype(_BF)
        s = lax.dot_general(q, kband, (((1,), (1,)), ((), ())), preferred_element_type=_F32)
        s = s * scale + bias
        m = jnp.max(s, axis=-1, keepdims=True)
        p = jnp.exp(s - m)
        l = jnp.sum(p, axis=-1, keepdims=True)
        pv = jnp.dot(p.astype(_BF), vband, preferred_element_type=_F32)
        mb = jnp.broadcast_to(m, (QBLK, HEAD_DIM))
        lb = jnp.broadcast_to(l, (QBLK, HEAD_DIM))
        if d > 1:
            idx = (g, pl.ds(start, QBLK, stride=d), slice(None))
        else:
            idx = (g, pl.ds(start, QBLK), slice(None))
        m_s[idx] = mb
        l_s[idx] = lb
        acc_s[idx] = pv

    def rows(ref, start, size, d):
        if d > 1:
            return ref[pl.ds(start, size, stride=d), :].astype(_BF)
        return ref[pl.ds(start, size), :].astype(_BF)

    for g, d in enumerate(B_DILATIONS):
        q_ref = (q0_ref, q1_ref, q2_ref)[g]
        span = QBLK * d
        nloc = ATT_T // span

        def first_block(r, g=g, d=d, q_ref=q_ref, span=span, nloc=nloc):
            pstart = (nloc - 1) * span + r
            kband = jnp.concatenate([rows(kp_ref, pstart, QBLK, d), rows(kc_ref, r, QBLK, d)], axis=0)
            vband = jnp.concatenate([rows(vp_ref, pstart, QBLK, d), rows(vc_ref, r, QBLK, d)], axis=0)
            block(g, d, q_ref, r, kband, vband, bias_s[n_groups + g])

        def later_block(r, n, g=g, d=d, q_ref=q_ref, span=span):
            bstart = (n - 1) * span + r
            kband = rows(kc_ref, bstart, 2 * QBLK, d)
            vband = rows(vc_ref, bstart, 2 * QBLK, d)
            block(g, d, q_ref, n * span + r, kband, vband, bias_s[g])

        if nloc == 1:
            def body(r, c, first_block=first_block):
                first_block(r)
                return c
            lax.fori_loop(0, d, body, 0, unroll=8)
        elif d == 1:
            first_block(0)

            def body(n, c, later_block=later_block):
                later_block(0, n)
                return c
            half = (nloc + 1) // 2
            lax.fori_loop(1, half + 1, body, 0, unroll=half)
            lax.fori_loop(half + 1, nloc, body, 0, unroll=nloc - half - 1)
        else:
            def body(r, c, first_block=first_block, later_block=later_block, nloc=nloc):
                first_block(r)
                for n in range(1, nloc):
                    later_block(r, n)
                return c
            lax.fori_loop(0, d, body, 0, unroll=d)

    m0, m1, m2 = m_s[0], m_s[1], m_s[2]
    mm = jnp.maximum(jnp.maximum(m0, m1), m2)
    w0 = jnp.exp(m0 - mm)
    w1 = jnp.exp(m1 - mm)
    w2 = jnp.exp(m2 - mm)
    num = w0 * acc_s[0] + w1 * acc_s[1] + w2 * acc_s[2]
    den = w0 * l_s[0] + w1 * l_s[1] + w2 * l_s[2]
    o_ref[...] = (num / den).astype(o_ref.dtype)


def _dilated_attention(q, kv, bsz, seq):
    n = q.shape[0]
    t = ATT_T
    tiles = seq // t
    hd = B_HEADS
    slopes = 2.0 ** (-8.0 * jnp.arange(1, hd + 1, dtype=_F32) / hd)

    def qspec(g):
        return pl.BlockSpec((t, HEAD_DIM), lambda b, j, h, s, g=g: (b * tiles + j, g * hd + h))

    def cur(off):
        return pl.BlockSpec((t, HEAD_DIM), lambda b, j, h, s: (b * tiles + j, off + h))

    def prev(off):
        return pl.BlockSpec((t, HEAD_DIM), lambda b, j, h, s: (b * tiles + jnp.maximum(j - 1, 0), off + h))

    grid_spec = pltpu.PrefetchScalarGridSpec(
        num_scalar_prefetch=1,
        grid=(bsz, tiles, hd),
        in_specs=[qspec(0), qspec(1), qspec(2), cur(0), prev(0), cur(hd), prev(hd)],
        out_specs=pl.BlockSpec((t, HEAD_DIM), lambda b, j, h, s: (b * tiles + j, h)),
        scratch_shapes=[
            pltpu.VMEM((2 * B_GROUPS, QBLK, 2 * QBLK), _F32),
            pltpu.VMEM((B_GROUPS, t, HEAD_DIM), _F32),
            pltpu.VMEM((B_GROUPS, t, HEAD_DIM), _F32),
            pltpu.VMEM((B_GROUPS, t, HEAD_DIM), _F32),
        ],
    )
    return pl.pallas_call(
        _attn_kernel,
        grid_spec=grid_spec,
        out_shape=jax.ShapeDtypeStruct((n, hd * HEAD_DIM), _BF),
        compiler_params=_params(("parallel", "parallel", "arbitrary")),
        name="dilated_attn",
    )(slopes, q, q, q, kv, kv, kv, kv)


def kernel(x, p, a_w_in, a_b_in, a_ln_g, a_ln_b, a_w_s, a_b_s, a_w_out, a_b_out, kv_w, b_w_q, b_w_o, ln1_g, ln1_b, ln2_g, ln2_b, router_w, router_b, moe_w_gu, moe_b_gu, moe_w_down, moe_b_down, ple_w_p, ple_w_g, ple_b_g):
    bsz, seq, d = x.shape
    n = bsz * seq
    xf = x.reshape(n, d)
    xb = xf.astype(_BF)
    row = lambda v: v.reshape(1, -1)
    kv = None
    for i in range(DEPTH):
        if i < N_A_LAYERS:
            gated = _gmlp_front(xb, a_w_in[i].astype(_BF), row(a_b_in[i]), row(a_ln_g[i]), row(a_ln_b[i]),
                                a_w_s[i], a_b_s[i].T)
            h, hb, idx, gates, rank, counts = _proj_ln(gated, a_w_out[i].astype(_BF), row(a_b_out[i]), xf,
                                         row(ln1_g[i]), row(ln1_b[i]), router_w[i], row(router_b[i]))
        else:
            jl = i - N_A_LAYERS
            q = _mm(xb, b_w_q[jl].astype(_BF))
            att = _dilated_attention(q, kv, bsz, seq)
            h, hb, idx, gates, rank, counts = _proj_ln(att, b_w_o[jl].astype(_BF), None, xf,
                                         row(ln1_g[i]), row(ln1_b[i]), router_w[i], row(router_b[i]))
        pos, slot_tok, ge, gs, nsub, tail = _routing(idx, rank, counts)
        y_sorted = _moe_experts(h, moe_w_gu, moe_b_gu, moe_w_down, moe_b_down, i,
                                slot_tok, ge, gs, nsub, tail)
        xf, xb = _combine_ln2(h, hb, p[i].reshape(n, PLE_DIM), gates, pos, y_sorted,
                              ple_w_g[i].astype(_BF), row(ple_b_g[i]), ple_w_p[i].astype(_BF),
                              row(ln2_g[i]), row(ln2_b[i]))
        if i == N_A_LAYERS - 1:
            kv = _mm(xb, kv_w.astype(_BF))
    return xf.reshape(bsz, seq, d)
```
